```python
import math
import jax, jax.numpy as jnp
from jax import lax
import numpy as np

D_MODEL = 1024
BATCH = 16
SEQ = 2048
DEPTH = 1

CHUNK = 64
N_META = 16
D_MIX = D_MODEL
D_A = D_MIX // 2
HEAD_A = 64
H_A = D_A // HEAD_A
LORA_W = 64
LORA_A = 64
LORA_G = 128
DECAY_SCALE = math.exp(-0.5)
GN_EPS_A = 64e-5
D_B = D_MIX - D_A
H_B = 4
HEAD_B = D_B // H_B
ROPE_BASE = 10000.0
GN_EPS_B = 1e-5
P_A = 3 * D_A + LORA_W + LORA_A + LORA_G
P_B = 4 * D_B
P_IN = P_A + P_B
N_GROUPS = 4
EXPERTS_PER_GROUP = 8
N_EXPERTS = N_GROUPS * EXPERTS_PER_GROUP
TOP_K = 2
D_EXPERT = 512
MOE_BLOCK = 128
NORM_EPS = 1e-6

kernel_name = 'hybrid_rwkv7_retention_hmoe'


def rmsnorm(x, g):
    xf = x.astype(jnp.float32)
    y = xf * lax.rsqrt(jnp.mean(xf * xf, axis=-1, keepdims=True) + NORM_EPS)
    return (y * g.astype(jnp.float32)).astype(x.dtype)


def head_norm(y, eps):
    mean = jnp.mean(y, axis=-1, keepdims=True)
    var = jnp.mean(jnp.square(y - mean), axis=-1, keepdims=True)
    return (y - mean) * lax.rsqrt(var + eps)


def rwkv7_mix(z, mu, decay_w0, decay_up, iclr_a0, iclr_up, gate_up, k_k, k_a, r_k, ln_w, ln_b):
    zf = z.astype(jnp.float32)
    z_prev = jnp.pad(zf, ((0, 0), (1, 0), (0, 0)))[:, :-1]
    zs = zf + (z_prev - zf) * mu
    offs = [D_A, 2 * D_A, 3 * D_A, 3 * D_A + LORA_W, 3 * D_A + LORA_W + LORA_A]
    r, k, v, w_lo, a_lo, g_lo = jnp.split(zs, offs, axis=-1)
    log_w = -DECAY_SCALE * jax.nn.sigmoid(decay_w0 + jnp.tanh(w_lo) @ decay_up)
    a = jax.nn.sigmoid(iclr_a0 + a_lo @ iclr_up)
    g = jax.nn.sigmoid(g_lo) @ gate_up
    B_, L_, _ = r.shape
    hs = lambda t: t.reshape(B_, L_, H_A, HEAD_A)
    r, k, v, a, log_w = hs(r), hs(k), hs(v), hs(a), hs(log_w)
    kk = k * k_k.reshape(H_A, HEAD_A)
    kk = kk / jnp.maximum(jnp.sqrt(jnp.sum(kk * kk, axis=-1, keepdims=True)), 1e-12)
    k = k * (1.0 + (a - 1.0) * k_a.reshape(H_A, HEAD_A))
    w = jnp.exp(log_w)

    def step(S, inp):
        r_t, w_t, k_t, v_t, kk_t, b_t = inp
        S = (S * w_t[:, :, None, :]
             + jnp.einsum('bhvk,bhk->bhv', S, -kk_t)[..., None] * b_t[:, :, None, :]
             + v_t[..., None] * k_t[:, :, None, :])
        return S, jnp.einsum('bhvk,bhk->bhv', S, r_t)

    xs = tuple(jnp.moveaxis(t, 1, 0) for t in (r, w, k, v, kk, kk * a))
    S0 = jnp.zeros((B_, H_A, HEAD_A, HEAD_A), jnp.float32)
    _, y = lax.scan(step, S0, xs)
    y = jnp.moveaxis(y, 0, 1)
    y = head_norm(y, GN_EPS_A).reshape(B_, L_, D_A) * ln_w + ln_b
    bonus = jnp.sum(r * k * r_k.reshape(H_A, HEAD_A), axis=-1, keepdims=True) * v
    return (y + bonus.reshape(B_, L_, D_A)) * g


def retention_mix(z, gn_w):
    zf = z.astype(jnp.float32)
    q, k, v, g = jnp.split(zf, 4, axis=-1)
    B_, L_, _ = q.shape
    q = q.reshape(B_, L_, H_B, HEAD_B)
    k = k.reshape(B_, L_, H_B, HEAD_B)
    v = v.reshape(B_, L_, H_B, HEAD_B)
    pos = jnp.arange(L_, dtype=jnp.float32)
    inv_freq = ROPE_BASE ** (-jnp.arange(0, HEAD_B, 2, dtype=jnp.float32) / HEAD_B)
    ang = pos[:, None] * inv_freq[None, :]
    cos = jnp.cos(ang)[None, :, None, :]
    sin = jnp.sin(ang)[None, :, None, :]

    def rope(t):
        t1, t2 = jnp.split(t, 2, axis=-1)
        return jnp.concatenate([t1 * cos - t2 * sin, t1 * sin + t2 * cos], axis=-1)

    q = rope(q)
    k = rope(k) * (HEAD_B ** -0.5)
    pad = (-L_) % CHUNK
    n_chunks = (L_ + pad) // CHUNK

    def chunked(t):
        t = jnp.pad(t, ((0, 0), (pad, 0), (0, 0), (0, 0)))
        return t.reshape(B_, n_chunks, CHUNK, H_B, HEAD_B).transpose(1, 0, 3, 2, 4)

    qc, kc, vc = chunked(q), chunked(k), chunked(v)
    log_gamma = jnp.log1p(-jnp.exp2(-5.0 - jnp.arange(H_B, dtype=jnp.float32)))
    idx = jnp.arange(CHUNK)
    rel = idx[:, None] - idx[None, :]
    decay_in = jnp.where(rel >= 0, jnp.exp(log_gamma[:, None, None] * jnp.maximum(rel, 0).astype(jnp.float32)), 0.0)
    q_decay = jnp.exp(log_gamma[:, None] * (idx + 1).astype(jnp.float32))
    k_decay = jnp.exp(log_gamma[:, None] * (CHUNK - 1 - idx).astype(jnp.float32))
    chunk_decay = jnp.exp(log_gamma * CHUNK)
    scores = jnp.einsum('nbhcd,nbhmd->nbhcm', qc, kc) * decay_in
    inner = jnp.einsum('nbhcm,nbhme->nbhce', scores, vc)

    def step(R, inp):
        q_c, k_c, v_c = inp
        cross = jnp.einsum('bhcd,bhde->bhce', q_c, R) * q_decay[None, :, :, None]
        R = R * chunk_decay[None, :, None, None] + jnp.einsum('bhmd,bhme->bhde', k_c * k_decay[None, :, :, None], v_c)
        return R, cross

    R0 = jnp.zeros((B_, H_B, HEAD_B, HEAD_B), jnp.float32)
    _, cross = lax.scan(step, R0, (qc, kc, vc))
    y = (inner + cross).transpose(1, 0, 3, 2, 4).reshape(B_, n_chunks * CHUNK, H_B, HEAD_B)[:, pad:]
    y = head_norm(y, GN_EPS_B).reshape(B_, L_, D_B) * gn_w
    return y * jax.nn.silu(g)


def hier_moe(u, w_rg, b_rg, w_re, b_re, w_gate, w_up, w_down):
    B_, L_, D_ = u.shape
    T = B_ * L_
    uf = u.reshape(T, D_)
    group_prob = jax.nn.softmax((uf @ w_rg).astype(jnp.float32) + b_rg.astype(jnp.float32), axis=-1)
    g_p, g_idx = lax.top_k(group_prob, 1)
    exp_logits = ((uf @ w_re).astype(jnp.float32) + b_re.astype(jnp.float32)).reshape(T, N_GROUPS, EXPERTS_PER_GROUP)
    exp_logits = exp_logits[jnp.arange(T), g_idx[:, 0]]
    e_p, e_idx = lax.top_k(jax.nn.softmax(exp_logits, axis=-1), TOP_K)
    weight = g_p * e_p / jnp.sum(e_p, axis=-1, keepdims=True)
    expert = g_idx * EXPERTS_PER_GROUP + e_idx
    n_assign = T * TOP_K
    e_flat = expert.reshape(-1).astype(jnp.int32)
    w_flat = weight.reshape(-1)
    tok_flat = jnp.repeat(jnp.arange(T, dtype=jnp.int32), TOP_K)
    order = jnp.argsort(e_flat)
    e_s, w_s, tok_s = e_flat[order], w_flat[order], tok_flat[order]
    counts = jnp.zeros((N_EXPERTS,), jnp.int32).at[e_flat].add(1)
    padded = (counts + MOE_BLOCK - 1) // MOE_BLOCK * MOE_BLOCK
    pad_end = jnp.cumsum(padded)
    pad_start = pad_end - padded
    raw_start = jnp.cumsum(counts) - counts
    dest = pad_start[e_s] + jnp.arange(n_assign, dtype=jnp.int32) - raw_start[e_s]
    n_blocks = -(-(n_assign + N_EXPERTS * (MOE_BLOCK - 1)) // MOE_BLOCK)
    n_rows = n_blocks * MOE_BLOCK
    row_tok = jnp.zeros((n_rows,), jnp.int32).at[dest].set(tok_s)
    row_w = jnp.zeros((n_rows,), jnp.float32).at[dest].set(w_s)
    block_expert = jnp.minimum(jnp.searchsorted(pad_end, jnp.arange(n_blocks, dtype=jnp.int32) * MOE_BLOCK, side='right'), N_EXPERTS - 1)
    x_rows = uf[row_tok].reshape(n_blocks, MOE_BLOCK, D_)

    def run_block(args):
        xb, e = args
        hid = jax.nn.silu(xb @ w_gate[e]) * (xb @ w_up[e])
        return hid @ w_down[e]

    y_rows = lax.map(run_block, (x_rows, block_expert)).reshape(n_rows, D_)
    out = jnp.zeros((T, D_), u.dtype).at[row_tok].add(y_rows * row_w[:, None].astype(u.dtype))
    return out.reshape(B_, L_, D_)


def setup_inputs(seed: int = 0) -> dict:
    key = jax.random.key(seed)
    ks = iter(jax.random.split(key, 32))
    f32 = jnp.float32
    nrm = lambda shape, scale: jax.random.normal(next(ks), shape, f32) * scale
    D = D_MODEL
    return {
        'x': nrm((BATCH, SEQ, D), 1.0),
        'meta_tokens': nrm((N_META, D), 1.0),
        'norm_mix': 1.0 + nrm((DEPTH, D), 0.02),
        'w_in': nrm((DEPTH, D, P_IN), D ** -0.5),
        'shift_mu': jax.random.uniform(next(ks), (DEPTH, P_A), f32, 0.1, 0.9),
        'decay_w0': jax.random.uniform(next(ks), (DEPTH, D_A), f32, -3.0, 3.0),
        'decay_up': nrm((DEPTH, LORA_W, D_A), 0.5 * LORA_W ** -0.5),
        'iclr_a0': nrm((DEPTH, D_A), 0.5),
        'iclr_up': nrm((DEPTH, LORA_A, D_A), 0.5 * LORA_A ** -0.5),
        'gate_up': nrm((DEPTH, LORA_G, D_A), LORA_G ** -0.5),
        'k_k': 0.85 + nrm((DEPTH, D_A), 0.05),
        'k_a': 1.0 + nrm((DEPTH, D_A), 0.05),
        'r_k': nrm((DEPTH, D_A), 0.1),
        'ln_w_a': 1.0 + nrm((DEPTH, D_A), 0.02),
        'ln_b_a': nrm((DEPTH, D_A), 0.02),
        'gn_w_b': 1.0 + nrm((DEPTH, D_B), 0.02),
        'w_out': nrm((DEPTH, D_MIX, D), D_MIX ** -0.5),
        'norm_ffn': 1.0 + nrm((DEPTH, D), 0.02),
        'router_group_w': nrm((DEPTH, D, N_GROUPS), D ** -0.5),
        'router_group_b': nrm((DEPTH, N_GROUPS), 0.01),
        'router_expert_w': nrm((DEPTH, D, N_EXPERTS), D ** -0.5),
        'router_expert_b': nrm((DEPTH, N_EXPERTS), 0.01),
        'moe_w_gate': nrm((DEPTH, N_EXPERTS, D, D_EXPERT), D ** -0.5),
        'moe_w_up': nrm((DEPTH, N_EXPERTS, D, D_EXPERT), D ** -0.5),
        'moe_w_down': nrm((DEPTH, N_EXPERTS, D_EXPERT, D), D_EXPERT ** -0.5),
        'norm_final': 1.0 + nrm((D,), 0.02),
    }


def reference(x, meta_tokens, norm_mix, w_in, shift_mu, decay_w0, decay_up, iclr_a0, iclr_up, gate_up,
              k_k, k_a, r_k, ln_w_a, ln_b_a, gn_w_b, w_out, norm_ffn, router_group_w, router_group_b,
              router_expert_w, router_expert_b, moe_w_gate, moe_w_up, moe_w_down, norm_final):
    B_ = x.shape[0]
    meta = jnp.broadcast_to(meta_tokens[None].astype(x.dtype), (B_, N_META, x.shape[-1]))
    h = jnp.concatenate([meta, x], axis=1)
    for i in range(DEPTH):
        u = rmsnorm(h, norm_mix[i])
        z = u @ w_in[i]
        y_a = rwkv7_mix(z[..., :P_A], shift_mu[i], decay_w0[i], decay_up[i], iclr_a0[i], iclr_up[i], gate_up[i],
                        k_k[i], k_a[i], r_k[i], ln_w_a[i], ln_b_a[i])
        y_b = retention_mix(z[..., P_A:], gn_w_b[i])
        mixed = jnp.concatenate([y_a, y_b], axis=-1).astype(h.dtype)
        h = h + mixed @ w_out[i]
        u = rmsnorm(h, norm_ffn[i])
        h = h + hier_moe(u, router_group_w[i], router_group_b[i], router_expert_w[i], router_expert_b[i],
                         moe_w_gate[i], moe_w_up[i], moe_w_down[i])
    return rmsnorm(h, norm_final)[:, N_META:]
```

```python
import functools
import math

import jax
import jax.numpy as jnp
from jax import lax
from jax.experimental import pallas as pl
from jax.experimental.pallas import tpu as pltpu

F32 = jnp.float32
BF16 = jnp.bfloat16

D_MODEL = 1024
CHUNK = 64
N_META = 16
D_A = 512
HEAD_A = 64
LORA_W = 64
LORA_A = 64
LORA_G = 128
DECAY_SCALE = math.exp(-0.5)
GN_EPS_A = 64e-5
D_B = 512
H_B = 4
HEAD_B = 128
ROPE_BASE = 10000.0
GN_EPS_B = 1e-5
P_A = 3 * D_A + LORA_W + LORA_A + LORA_G
P_B = 4 * D_B
P_IN = P_A + P_B
N_GROUPS = 4
EXPERTS_PER_GROUP = 8
N_EXPERTS = N_GROUPS * EXPERTS_PER_GROUP
D_EXPERT = 512
NORM_EPS = 1e-6

LANES = 128
N_PAIRS = D_A // LANES
VMEM_LIMIT = 48 * 1024 * 1024

TM_PROJ = 256
TM_ROUTE = 512
TM_COMB = 256
BM_MOE = 256
ROUTE_LANES = LANES


def _dot(a, b):
    return jnp.dot(a, b, preferred_element_type=F32)


def _dot_nt(a, b):
    return lax.dot_general(a, b, (((1,), (1,)), ((), ())), preferred_element_type=F32)


def _dot_tn(a, b):
    return lax.dot_general(a, b, (((0,), (0,)), ((), ())), preferred_element_type=F32)


def _split2(x):
    hi = x.astype(BF16)
    lo = (x - hi.astype(F32)).astype(BF16)
    return hi, lo


def _split3(x):
    x1 = x.astype(BF16)
    r1 = x - x1.astype(F32)
    x2 = r1.astype(BF16)
    x3 = (r1 - x2.astype(F32)).astype(BF16)
    return x1, x2, x3


def _dot_x2(x, w_exact):
    hi, lo = _split2(x)
    return _dot(hi, w_exact) + _dot(lo, w_exact)


def _dot_3pass(x, w_hi, w_lo):
    hi, lo = _split2(x)
    return _dot(hi, w_hi) + _dot(lo, w_hi) + _dot(hi, w_lo)


def _rms(x, g):
    return x * lax.rsqrt(jnp.mean(x * x, axis=-1, keepdims=True) + NORM_EPS) * g


def _in_proj_kernel(x_ref, g_ref, w_ref, z_ref):
    u = _rms(x_ref[...], g_ref[...])
    z_ref[...] = _dot(u.astype(BF16), w_ref[...])


def _in_proj(x2d, g, w_bf16):
    m, d = x2d.shape
    tm = min(TM_PROJ, m)
    n = w_bf16.shape[1]
    return pl.pallas_call(
        _in_proj_kernel,
        grid=(m // tm,),
        in_specs=[
            pl.BlockSpec((tm, d), lambda i: (i, 0)),
            pl.BlockSpec((1, d), lambda i: (0, 0)),
            pl.BlockSpec((d, n), lambda i: (0, 0)),
        ],
        out_specs=pl.BlockSpec((tm, n), lambda i: (i, 0)),
        out_shape=jax.ShapeDtypeStruct((m, n), F32),
        compiler_params=pltpu.CompilerParams(
            dimension_semantics=("arbitrary",), vmem_limit_bytes=VMEM_LIMIT),
        name="in_proj",
    )(x2d, g, w_bf16)


def _mixer_kernel(z_ref, zprev0_ref, sa0_ref, rb0_ref, cos_ref, sin_ref, mu_ref, pvec_ref,
                  wwa_hi_ref, wwa_lo_ref, wg_hi_ref, wg_lo_ref, bd_ref,
                  mixed_ref, sa_out_ref, rb_out_ref,
                  sa_scr, rb_scr, zlast_scr):
    c_idx = pl.program_id(1)
    n_chunks = pl.num_programs(1)
    C = CHUNK

    @pl.when(c_idx == 0)
    def _():
        sa_scr[...] = sa0_ref[...]
        rb_scr[...] = rb0_ref[...]
        zlast_scr[...] = zprev0_ref[...]

    z = z_ref[0]
    za = z[:, :P_A]
    row = lax.broadcasted_iota(jnp.int32, (C, 1), 0)
    zprev = jnp.where(row == 0, zlast_scr[...], pltpu.roll(za, 1, 0))
    zlast_scr[...] = za[C - 1:C, :]
    zs = za + (zprev - za) * mu_ref[...]

    r = zs[:, 0:D_A]
    k = zs[:, D_A:2 * D_A]
    v = zs[:, 2 * D_A:3 * D_A]
    wa = zs[:, 3 * D_A:3 * D_A + LORA_W + LORA_A]
    g_lo = zs[:, 3 * D_A + LORA_W + LORA_A:P_A]

    decay_w0 = pvec_ref[0:1, :]
    iclr_a0 = pvec_ref[1:2, :]
    k_k = pvec_ref[2:3, :]
    k_a = pvec_ref[3:4, :]
    r_k = pvec_ref[4:5, :]
    ln_w = pvec_ref[5:6, :]
    ln_b = pvec_ref[6:7, :]
    gn_w = pvec_ref[7:8, :]

    lane = lax.broadcasted_iota(jnp.int32, (1, LANES), 1)
    wa_act = jnp.where(lane < LORA_W, jnp.tanh(wa), wa)
    pre = _dot_3pass(wa_act, wwa_hi_ref[...], wwa_lo_ref[...])
    log_w = -DECAY_SCALE * jax.nn.sigmoid(decay_w0 + pre[:, :D_A])
    a = jax.nn.sigmoid(iclr_a0 + pre[:, D_A:])
    g = _dot_3pass(jax.nn.sigmoid(g_lo), wg_hi_ref[...], wg_lo_ref[...])

    bd = bd_ref[...]
    kk = k * k_k
    kk = kk / jnp.maximum(jnp.sqrt(_dot_x2(kk * kk, bd)), 1e-12)
    k2 = k * (1.0 + (a - 1.0) * k_a)
    b = kk * a

    ti = lax.broadcasted_iota(jnp.int32, (C, C), 0)
    sj = lax.broadcasted_iota(jnp.int32, (C, C), 1)
    tril = (sj <= ti).astype(BF16)
    lw1, lw2, lw3 = _split3(log_w)
    cum = _dot(tril, lw1) + _dot(tril, lw2) + _dot(tril, lw3)
    cmid = cum[C // 2 - 1:C // 2, :]
    cc = cum - cmid
    e_nc = jnp.exp(-cc)
    rg = r * jnp.exp(cc)
    kkg = kk * jnp.exp(cc - log_w)
    kinv = k2 * e_nc
    binv = b * e_nc
    e_mid = jnp.exp(cmid)
    e_end = jnp.exp(cc[C - 1:C, :])
    gam = jnp.exp(cum[C - 1:C, :])

    i2 = lax.broadcasted_iota(jnp.int32, (2 * C, 2 * C), 0)
    j2 = lax.broadcasted_iota(jnp.int32, (2 * C, 2 * C), 1)
    bi = i2 >= C
    bj = j2 >= C
    t2 = jnp.where(bi, i2 - C, i2)
    s2 = jnp.where(bj, j2 - C, j2)
    same_blk = bi == bj
    strict_same = (s2 < t2) & same_blk
    strict_cross = (s2 < t2) & jnp.logical_not(same_blk)
    ta = lax.broadcasted_iota(jnp.int32, (C, 2 * C), 0)
    ja = lax.broadcasted_iota(jnp.int32, (C, 2 * C), 1)
    incl = jnp.where(ja >= C, ja - C, ja) <= ta
    m0 = lane < HEAD_A
    m1 = jnp.logical_not(m0)

    y_parts = []
    for p in range(N_PAIRS):
        sl = slice(p * LANES, (p + 1) * LANES)
        rg_p, kkg_p, kinv_p, binv_p, v_p = rg[:, sl], kkg[:, sl], kinv[:, sl], binv[:, sl], v[:, sl]
        kkg0 = jnp.where(m0, kkg_p, 0.0)
        kkg1 = jnp.where(m1, kkg_p, 0.0)
        l0 = jnp.concatenate([kkg0, jnp.where(m0, rg_p, 0.0)], axis=0).astype(BF16)
        l1 = jnp.concatenate([kkg1, jnp.where(m1, rg_p, 0.0)], axis=0).astype(BF16)
        kinv_b = kinv_p.astype(BF16)
        binv_b = binv_p.astype(BF16)
        out0 = _dot_nt(l0, jnp.concatenate([binv_b, kinv_b], axis=0))
        out1 = _dot_nt(l1, jnp.concatenate([kinv_b, binv_b], axis=0))
        top = jnp.concatenate([out0[:C], out1[:C]], axis=0)
        n_bd = jnp.where(strict_same, -top, 0.0)
        q_anti = jnp.where(strict_cross, top, 0.0)
        vm0 = jnp.where(m0, v_p, 0.0)
        vm1 = jnp.where(m1, v_p, 0.0)
        qv = _dot(q_anti.astype(BF16), jnp.concatenate([vm1, vm0], axis=0).astype(BF16))
        x = jnp.concatenate([qv, jnp.concatenate([kkg0, kkg1], axis=0)], axis=1)
        npow = n_bd
        for it in range(6):
            nb = npow.astype(BF16)
            x = x + _dot(nb, x.astype(BF16))
            if it < 5:
                npow = _dot(nb, nb)
        w_stack = x[:, :LANES]
        kkt_stack = x[:, LANES:]

        s_p = sa_scr[p]
        s0m = (s_p * e_mid[:, sl]).astype(BF16)
        u_stack = _dot_nt(kkt_stack.astype(BF16), s0m) + w_stack
        y_p = _dot_nt(rg_p.astype(BF16), s0m)
        a0 = jnp.where(incl, out0[C:], 0.0).astype(BF16)
        a1 = jnp.where(incl, out1[C:], 0.0).astype(BF16)
        y_p = y_p + _dot(a0, jnp.concatenate([-u_stack[:C], vm0], axis=0).astype(BF16))
        y_p = y_p + _dot(a1, jnp.concatenate([vm1, -u_stack[C:]], axis=0).astype(BF16))
        y_parts.append(y_p)

        u_pair = u_stack[:C] + u_stack[C:]
        gt = _dot_tn(jnp.concatenate([v_p, -u_pair], axis=0).astype(BF16),
                     jnp.concatenate([kinv_b, binv_b], axis=0))
        gt = jnp.where(same_blk, gt, 0.0)
        sa_scr[p] = s_p * gam[:, sl] + gt * e_end[:, sl]

    y = jnp.concatenate(y_parts, axis=1)
    inv_n = 1.0 / HEAD_A
    mean = _dot_x2(y, bd) * inv_n
    dlt = y - mean
    var = _dot_x2(dlt * dlt, bd) * inv_n
    yn = dlt * lax.rsqrt(var + GN_EPS_A)
    bonus = _dot_x2(r * k2 * r_k, bd) * v
    out_a = (yn * ln_w + ln_b + bonus) * g

    zb = z[:, P_A:]
    cosf = cos_ref[...]
    sinf = sin_ref[...]
    relf = (ti - sj).astype(F32)
    causal = sj <= ti
    rowf = row.astype(F32)
    outs_b = []
    for h in range(H_B):
        sl = slice(h * HEAD_B, (h + 1) * HEAD_B)
        lg = math.log1p(-(2.0 ** (-5.0 - h)))
        q_h = zb[:, sl]
        k_h = zb[:, D_B + h * HEAD_B:D_B + (h + 1) * HEAD_B]
        v_h = zb[:, 2 * D_B + h * HEAD_B:2 * D_B + (h + 1) * HEAD_B]
        g_h = zb[:, 3 * D_B + h * HEAD_B:3 * D_B + (h + 1) * HEAD_B]
        q_r = q_h * cosf + pltpu.roll(q_h, HEAD_B // 2, 1) * sinf
        k_r = (k_h * cosf + pltpu.roll(k_h, HEAD_B // 2, 1) * sinf) * (HEAD_B ** -0.5)
        decay_in = jnp.where(causal, jnp.exp(lg * jnp.maximum(relf, 0.0)), 0.0)
        q_b = q_r.astype(BF16)
        v_b = v_h.astype(BF16)
        scores = _dot_nt(q_b, k_r.astype(BF16)) * decay_in
        inner = _dot(scores.astype(BF16), v_b)
        r_h = rb_scr[h]
        cross = _dot(q_b, r_h.astype(BF16)) * jnp.exp(lg * (rowf + 1.0))
        k_dec = k_r * jnp.exp(lg * (float(C - 1) - rowf))
        rb_scr[h] = r_h * math.exp(lg * C) + _dot_tn(k_dec.astype(BF16), v_b)
        y_h = inner + cross
        mu_h = jnp.mean(y_h, axis=-1, keepdims=True)
        d_h = y_h - mu_h
        var_h = jnp.mean(d_h * d_h, axis=-1, keepdims=True)
        yn_h = d_h * lax.rsqrt(var_h + GN_EPS_B)
        outs_b.append(yn_h * gn_w[:, sl] * (g_h * jax.nn.sigmoid(g_h)))
    out_b = jnp.concatenate(outs_b, axis=1)

    mixed_ref[0] = jnp.concatenate([out_a, out_b], axis=1).astype(mixed_ref.dtype)

    @pl.when(c_idx == n_chunks - 1)
    def _():
        sa_out_ref[0] = sa_scr[...]
        rb_out_ref[0] = rb_scr[...]


def _mixer(z3, zprev0, sa0, rb0, cosf, sinf, mu, pvec, wwa_hi, wwa_lo, wg_hi, wg_lo, bd):
    bsz, length, _ = z3.shape
    n_chunks = length // CHUNK
    const2 = lambda b, c: (0, 0)
    const3 = lambda b, c: (0, 0, 0)
    st_shape = (N_PAIRS, LANES, LANES)
    return pl.pallas_call(
        _mixer_kernel,
        grid=(bsz, n_chunks),
        in_specs=[
            pl.BlockSpec((1, CHUNK, P_IN), lambda b, c: (b, c, 0)),
            pl.BlockSpec((1, P_A), const2),
            pl.BlockSpec(st_shape, const3),
            pl.BlockSpec((H_B, HEAD_B, HEAD_B), const3),
            pl.BlockSpec((CHUNK, HEAD_B), lambda b, c: (c, 0)),
            pl.BlockSpec((CHUNK, HEAD_B), lambda b, c: (c, 0)),
            pl.BlockSpec((1, P_A), const2),
            pl.BlockSpec((8, D_A), const2),
            pl.BlockSpec((LORA_W + LORA_A, 2 * D_A), const2),
            pl.BlockSpec((LORA_W + LORA_A, 2 * D_A), const2),
            pl.BlockSpec((LORA_G, D_A), const2),
            pl.BlockSpec((LORA_G, D_A), const2),
            pl.BlockSpec((D_A, D_A), const2),
        ],
        out_specs=[
            pl.BlockSpec((1, CHUNK, D_A + D_B), lambda b, c: (b, c, 0)),
            pl.BlockSpec((1,) + st_shape, lambda b, c: (b, 0, 0, 0)),
            pl.BlockSpec((1, H_B, HEAD_B, HEAD_B), lambda b, c: (b, 0, 0, 0)),
        ],
        out_shape=[
            jax.ShapeDtypeStruct((bsz, length, D_A + D_B), BF16),
            jax.ShapeDtypeStruct((bsz,) + st_shape, F32),
            jax.ShapeDtypeStruct((bsz, H_B, HEAD_B, HEAD_B), F32),
        ],
        scratch_shapes=[
            pltpu.VMEM(st_shape, F32),
            pltpu.VMEM((H_B, HEAD_B, HEAD_B), F32),
            pltpu.VMEM((1, P_A), F32),
        ],
        compiler_params=pltpu.CompilerParams(
            dimension_semantics=("arbitrary", "arbitrary"), vmem_limit_bytes=VMEM_LIMIT),
        name="mixer",
    )(z3, zprev0, sa0, rb0, cosf, sinf, mu, pvec, wwa_hi, wwa_lo, wg_hi, wg_lo, bd)


def _out_router_kernel(mixed_ref, x_ref, wout_ref, g_ref, wr_hi_ref, wr_lo_ref, br_ref, ltri_ref,
                       h1_ref, u2_ref, route_ref, counts_ref, cnt_scr):
    i = pl.program_id(0)

    @pl.when(i == 0)
    def _():
        cnt_scr[...] = jnp.zeros_like(cnt_scr)

    h1 = x_ref[...] + _dot(mixed_ref[...], wout_ref[...])
    h1_ref[...] = h1
    u2 = _rms(h1, g_ref[...])
    u2_ref[...] = u2
    logits = _dot_3pass(u2, wr_hi_ref[...], wr_lo_ref[...]) + br_ref[...]

    tm = logits.shape[0]
    lane = lax.broadcasted_iota(jnp.int32, (tm, ROUTE_LANES), 1).astype(F32)
    neg = -jnp.inf
    big = float(ROUTE_LANES)
    first = float(N_GROUPS)

    def rmax(t):
        return jnp.max(t, axis=-1, keepdims=True)

    def rsum(t):
        return jnp.sum(t, axis=-1, keepdims=True)

    def rmin(t):
        return jnp.min(t, axis=-1, keepdims=True)

    gmask = lane < N_GROUPS
    gmax = rmax(jnp.where(gmask, logits, neg))
    gexp = jnp.where(gmask, jnp.exp(logits - gmax), 0.0)
    gprob = gexp / rsum(gexp)
    g_p = rmax(gprob)
    g_idx = rmin(jnp.where(gmask & (gprob == g_p), lane, big))

    lo_lane = first + EXPERTS_PER_GROUP * g_idx
    emask = (lane >= lo_lane) & (lane < lo_lane + EXPERTS_PER_GROUP)
    emax = rmax(jnp.where(emask, logits, neg))
    eexp = jnp.where(emask, jnp.exp(logits - emax), 0.0)
    eprob = jnp.where(emask, eexp / rsum(eexp), -1.0)
    p1 = rmax(eprob)
    i1 = rmin(jnp.where(eprob == p1, lane, big))
    eprob2 = jnp.where(lane == i1, -1.0, eprob)
    p2 = rmax(eprob2)
    i2 = rmin(jnp.where(eprob2 == p2, lane, big))
    w1 = g_p * p1 / (p1 + p2)
    w2 = g_p * p2 / (p1 + p2)

    sel1 = lane == i1
    sel2 = lane == i2
    onehot = jnp.where(sel1 | sel2, 1.0, 0.0)
    before = _dot(ltri_ref[...], onehot.astype(BF16)) + cnt_scr[...]
    rank1 = rsum(jnp.where(sel1, before, 0.0))
    rank2 = rsum(jnp.where(sel2, before, 0.0))
    cnt_scr[...] = cnt_scr[...] + jnp.sum(onehot, axis=0, keepdims=True)
    counts_ref[...] = cnt_scr[...]

    rec = jnp.where(lane == 0, i1 - first, 0.0)
    rec = jnp.where(lane == 1, i2 - first, rec)
    rec = jnp.where(lane == 2, w1, rec)
    rec = jnp.where(lane == 3, w2, rec)
    rec = jnp.where(lane == 4, rank1, rec)
    rec = jnp.where(lane == 5, rank2, rec)
    route_ref[...] = rec


def _out_router(mixed2d, x2d, wout_bf16, g, wr_hi, wr_lo, br, ltri):
    t, d = x2d.shape
    tm = ltri.shape[0]
    c2 = lambda i: (0, 0)
    return pl.pallas_call(
        _out_router_kernel,
        grid=(t // tm,),
        in_specs=[
            pl.BlockSpec((tm, d), lambda i: (i, 0)),
            pl.BlockSpec((tm, d), lambda i: (i, 0)),
            pl.BlockSpec((d, d), c2),
            pl.BlockSpec((1, d), c2),
            pl.BlockSpec((d, ROUTE_LANES), c2),
            pl.BlockSpec((d, ROUTE_LANES), c2),
            pl.BlockSpec((1, ROUTE_LANES), c2),
            pl.BlockSpec((tm, tm), c2),
        ],
        out_specs=[
            pl.BlockSpec((tm, d), lambda i: (i, 0)),
            pl.BlockSpec((tm, d), lambda i: (i, 0)),
            pl.BlockSpec((tm, ROUTE_LANES), lambda i: (i, 0)),
            pl.BlockSpec((1, ROUTE_LANES), c2),
        ],
        out_shape=[
            jax.ShapeDtypeStruct((t, d), F32),
            jax.ShapeDtypeStruct((t, d), F32),
            jax.ShapeDtypeStruct((t, ROUTE_LANES), F32),
            jax.ShapeDtypeStruct((1, ROUTE_LANES), F32),
        ],
        scratch_shapes=[pltpu.VMEM((1, ROUTE_LANES), F32)],
        compiler_params=pltpu.CompilerParams(
            dimension_semantics=("arbitrary",), vmem_limit_bytes=VMEM_LIMIT),
        name="out_router",
    )(mixed2d, x2d, wout_bf16, g, wr_hi, wr_lo, br, ltri)


def _row_copy(src_ref, src_row, dst_ref, dst_row, sem):
    return pltpu.make_async_copy(src_ref.at[pl.ds(src_row, 1), :], dst_ref.at[pl.ds(dst_row, 1), :], sem)


def _dispatch_kernel(pad_end_ref, padded_ref, dest_ref, u2_ref, xrows_ref, zero_scr, zsem, sem):
    i = pl.program_id(0)
    tm = u2_ref.shape[0]
    n_blocks = xrows_ref.shape[0] // BM_MOE
    n_used = pad_end_ref[N_EXPERTS - 1] // BM_MOE

    def zero_block(start):
        return pltpu.make_async_copy(zero_scr, xrows_ref.at[pl.ds(pl.multiple_of(start, BM_MOE), BM_MOE), :], zsem)

    def zero_copy(e):
        return zero_block(pad_end_ref[e] - BM_MOE)

    @pl.when(i == 0)
    def _():
        zero_scr[...] = jnp.zeros_like(zero_scr)
        for e in range(N_EXPERTS):
            @pl.when(padded_ref[e] > 0)
            def _():
                zero_copy(e).start()

        def tail_start(j, carry):
            zero_block(j * BM_MOE).start()
            return carry

        lax.fori_loop(n_used, n_blocks, tail_start, 0)
        for e in range(N_EXPERTS):
            @pl.when(padded_ref[e] > 0)
            def _():
                zero_copy(e).wait()

        def tail_wait(j, carry):
            zero_block(j * BM_MOE).wait()
            return carry

        lax.fori_loop(n_used, n_blocks, tail_wait, 0)

    def issue(j, carry):
        _row_copy(u2_ref, j, xrows_ref, dest_ref[0, 0, 2 * j], sem).start()
        _row_copy(u2_ref, j, xrows_ref, dest_ref[0, 0, 2 * j + 1], sem).start()
        return carry

    lax.fori_loop(0, tm, issue, 0)

    def drain(j, carry):
        _row_copy(u2_ref, j, xrows_ref, dest_ref[0, 0, 2 * j], sem).wait()
        _row_copy(u2_ref, j, xrows_ref, dest_ref[0, 0, 2 * j + 1], sem).wait()
        return carry

    lax.fori_loop(0, tm, drain, 0)


def _dispatch(pad_end, padded, dest3, u2, n_rows):
    t, d = u2.shape
    tm = dest3.shape[2] // 2
    grid_spec = pltpu.PrefetchScalarGridSpec(
        num_scalar_prefetch=2,
        grid=(t // tm,),
        in_specs=[
            pl.BlockSpec((1, 1, 2 * tm), lambda i, pe, pd: (i, 0, 0), memory_space=pltpu.SMEM),
            pl.BlockSpec((tm, d), lambda i, pe, pd: (i, 0)),
        ],
        out_specs=pl.BlockSpec(memory_space=pl.ANY),
        scratch_shapes=[
            pltpu.VMEM((BM_MOE, d), F32),
            pltpu.SemaphoreType.DMA(()),
            pltpu.SemaphoreType.DMA(()),
        ],
    )
    return pl.pallas_call(
        _dispatch_kernel,
        grid_spec=grid_spec,
        out_shape=jax.ShapeDtypeStruct((n_rows, d), F32),
        compiler_params=pltpu.CompilerParams(
            dimension_semantics=("arbitrary",), vmem_limit_bytes=VMEM_LIMIT),
        name="dispatch",
    )(pad_end, padded, dest3, u2)


def _expert_ffn_kernel(be_ref, nused_ref, x_ref, wg_ref, wu_ref, wd_ref, y_ref):
    i = pl.program_id(0)

    @pl.when(i < nused_ref[0])
    def _():
        x = x_ref[...].astype(BF16)
        hg = _dot(x, wg_ref[0])
        hu = _dot(x, wu_ref[0])
        hid = (hg * jax.nn.sigmoid(hg)) * hu
        y_ref[...] = _dot(hid.astype(BF16), wd_ref[0])

    @pl.when(i >= nused_ref[0])
    def _():
        y_ref[...] = jnp.zeros_like(y_ref)


def _expert_ffn(block_expert, n_used, x_rows, wg, wu, wd):
    n_rows, d = x_rows.shape
    n_blocks = n_rows // BM_MOE
    de = wg.shape[2]

    def row_map(i, be, nu):
        return (jnp.minimum(i, nu[0] - 1), 0)

    def w_map(i, be, nu):
        return (be[jnp.minimum(i, nu[0] - 1)], 0, 0)

    grid_spec = pltpu.PrefetchScalarGridSpec(
        num_scalar_prefetch=2,
        grid=(n_blocks,),
        in_specs=[
            pl.BlockSpec((BM_MOE, d), row_map),
            pl.BlockSpec((1, d, de), w_map),
            pl.BlockSpec((1, d, de), w_map),
            pl.BlockSpec((1, de, d), w_map),
        ],
        out_specs=pl.BlockSpec((BM_MOE, d), lambda i, be, nu: (i, 0)),
    )
    return pl.pallas_call(
        _expert_ffn_kernel,
        grid_spec=grid_spec,
        out_shape=jax.ShapeDtypeStruct((n_rows, d), F32),
        compiler_params=pltpu.CompilerParams(
            dimension_semantics=("arbitrary",), vmem_limit_bytes=VMEM_LIMIT),
        name="expert_ffn",
    )(block_expert, n_used, x_rows, wg, wu, wd)


def _combine_kernel(dest_ref, h1_ref, route_ref, g_ref, yrows_ref, out_ref, ybuf, sem):
    tm = h1_ref.shape[0]

    def copies(j):
        return (_row_copy(yrows_ref, dest_ref[0, 0, 2 * j], ybuf.at[0], j, sem),
                _row_copy(yrows_ref, dest_ref[0, 0, 2 * j + 1], ybuf.at[1], j, sem))

    def issue(j, carry):
        c0, c1 = copies(j)
        c0.start()
        c1.start()
        return carry

    lax.fori_loop(0, tm, issue, 0)

    def drain(j, carry):
        c0, c1 = copies(j)
        c0.wait()
        c1.wait()
        return carry

    lax.fori_loop(0, tm, drain, 0)

    rec = route_ref[...]
    w1 = rec[:, 2:3]
    w2 = rec[:, 3:4]
    h = h1_ref[...] + ybuf[0] * w1 + ybuf[1] * w2
    out_ref[...] = _rms(h, g_ref[...])


def _combine(dest3, h1, route, g, y_rows):
    t, d = h1.shape
    tm = dest3.shape[2] // 2
    return pl.pallas_call(
        _combine_kernel,
        grid=(t // tm,),
        in_specs=[
            pl.BlockSpec((1, 1, 2 * tm), lambda i: (i, 0, 0), memory_space=pltpu.SMEM),
            pl.BlockSpec((tm, d), lambda i: (i, 0)),
            pl.BlockSpec((tm, ROUTE_LANES), lambda i: (i, 0)),
            pl.BlockSpec((1, d), lambda i: (0, 0)),
            pl.BlockSpec(memory_space=pl.ANY),
        ],
        out_specs=pl.BlockSpec((tm, d), lambda i: (i, 0)),
        out_shape=jax.ShapeDtypeStruct((t, d), F32),
        scratch_shapes=[pltpu.VMEM((2, tm, d), F32), pltpu.SemaphoreType.DMA(())],
        compiler_params=pltpu.CompilerParams(
            dimension_semantics=("arbitrary",), vmem_limit_bytes=VMEM_LIMIT),
        name="combine",
    )(dest3, h1, route, g, y_rows)


def _rope_tables(positions):
    inv_freq = ROPE_BASE ** (-jnp.arange(0, HEAD_B, 2, dtype=F32) / HEAD_B)
    ang = positions.astype(F32)[:, None] * inv_freq[None, :]
    cos, sin = jnp.cos(ang), jnp.sin(ang)
    return jnp.concatenate([cos, cos], axis=1), jnp.concatenate([-sin, sin], axis=1)


def _hi_lo(w):
    hi = w.astype(BF16)
    return hi, (w - hi.astype(F32)).astype(BF16)


def kernel(x, meta_tokens, norm_mix, w_in, shift_mu, decay_w0, decay_up, iclr_a0, iclr_up, gate_up, k_k, k_a, r_k, ln_w_a, ln_b_a, gn_w_b, w_out, norm_ffn, router_group_w, router_group_b, router_expert_w, router_expert_b, moe_w_gate, moe_w_up, moe_w_down, norm_final):
    bsz, seq, d = x.shape
    assert d == D_MODEL and seq % CHUNK == 0 and norm_mix.shape[0] == 1
    t = bsz * seq
    li = 0

    w_in_b = w_in[li].astype(BF16)
    g_mix = norm_mix[li][None, :]
    mu = shift_mu[li][None, :]
    pvec = jnp.stack([decay_w0[li], iclr_a0[li], k_k[li], k_a[li], r_k[li], ln_w_a[li], ln_b_a[li], gn_w_b[li]])
    wwa = jnp.zeros((LORA_W + LORA_A, 2 * D_A), F32)
    wwa = wwa.at[:LORA_W, :D_A].set(decay_up[li]).at[LORA_W:, D_A:].set(iclr_up[li])
    wwa_hi, wwa_lo = _hi_lo(wwa)
    wg_hi, wg_lo = _hi_lo(gate_up[li])
    ch = jnp.arange(D_A) // HEAD_A
    bd = (ch[:, None] == ch[None, :]).astype(BF16)

    meta_pad = jnp.concatenate([jnp.zeros((CHUNK - N_META, d), F32), meta_tokens.astype(F32)], axis=0)
    z_meta = _in_proj(meta_pad, g_mix, w_in_b)
    cos_m, sin_m = _rope_tables(jnp.arange(CHUNK) - (CHUNK - N_META))
    zeros_a = jnp.zeros((N_PAIRS, LANES, LANES), F32)
    zeros_b = jnp.zeros((H_B, HEAD_B, HEAD_B), F32)
    _, sa_meta, rb_meta = _mixer(z_meta[None], jnp.zeros((1, P_A), F32), zeros_a, zeros_b, cos_m, sin_m,
                                 mu, pvec, wwa_hi, wwa_lo, wg_hi, wg_lo, bd)

    x2d = x.reshape(t, d)
    z = _in_proj(x2d, g_mix, w_in_b)
    cos_x, sin_x = _rope_tables(N_META + jnp.arange(seq))
    mixed, _, _ = _mixer(z.reshape(bsz, seq, P_IN), z_meta[CHUNK - 1:CHUNK, :P_A], sa_meta[0], rb_meta[0],
                         cos_x, sin_x, mu, pvec, wwa_hi, wwa_lo, wg_hi, wg_lo, bd)

    tm_r = min(TM_ROUTE, t)
    wr = jnp.zeros((d, ROUTE_LANES), F32)
    wr = wr.at[:, :N_GROUPS].set(router_group_w[li]).at[:, N_GROUPS:N_GROUPS + N_EXPERTS].set(router_expert_w[li])
    wr_hi, wr_lo = _hi_lo(wr)
    br = jnp.zeros((1, ROUTE_LANES), F32)
    br = br.at[0, :N_GROUPS].set(router_group_b[li]).at[0, N_GROUPS:N_GROUPS + N_EXPERTS].set(router_expert_b[li])
    ii = jnp.arange(tm_r)
    ltri = (ii[None, :] < ii[:, None]).astype(BF16)
    h1, u2, route, counts = _out_router(mixed.reshape(t, d), x2d, w_out[li].astype(BF16), norm_ffn[li][None, :],
                                        wr_hi, wr_lo, br, ltri)

    n_blocks = -(-(2 * t + N_EXPERTS * (BM_MOE - 1)) // BM_MOE)
    n_rows = n_blocks * BM_MOE
    cnt = counts[0, N_GROUPS:N_GROUPS + N_EXPERTS].astype(jnp.int32)
    padded = (cnt + BM_MOE - 1) // BM_MOE * BM_MOE
    pad_end = jnp.cumsum(padded)
    pad_start = pad_end - padded
    n_used = (pad_end[-1:] // BM_MOE).astype(jnp.int32)
    block_expert = jnp.minimum(
        jnp.searchsorted(pad_end, jnp.arange(n_blocks, dtype=jnp.int32) * BM_MOE, side='right'),
        N_EXPERTS - 1).astype(jnp.int32)
    eids = route[:, 0:2].astype(jnp.int32)
    ranks = route[:, 4:6].astype(jnp.int32)
    dest = pad_start[eids] + ranks

    x_rows = _dispatch(pad_end.astype(jnp.int32), padded.astype(jnp.int32),
                       dest.reshape(t // tm_r, 1, 2 * tm_r), u2, n_rows)
    y_rows = _expert_ffn(block_expert, n_used, x_rows, moe_w_gate[li].astype(BF16),
                         moe_w_up[li].astype(BF16), moe_w_down[li].astype(BF16))
    tm_c = min(TM_COMB, t)
    out = _combine(dest.reshape(t // tm_c, 1, 2 * tm_c), h1, route, norm_final[None, :], y_rows)
    return out.reshape(bsz, seq, d)
```

```python
import functools
import math

import jax
import jax.numpy as jnp
from jax import lax
from jax.experimental import pallas as pl
from jax.experimental.pallas import tpu as pltpu

F32 = jnp.float32
BF16 = jnp.bfloat16

D_MODEL = 1024
CHUNK = 64
LOG2_CHUNK = 6
N_META = 16
D_A = 512
HEAD_A = 64
LORA_W = 64
LORA_A = 64
LORA_G = 128
DECAY_SCALE = math.exp(-0.5)
GN_EPS_A = 64e-5
D_B = 512
H_B = 4
HEAD_B = 128
ROPE_BASE = 10000.0
GN_EPS_B = 1e-5
P_A = 3 * D_A + LORA_W + LORA_A + LORA_G
P_B = 4 * D_B
P_IN = P_A + P_B
N_GROUPS = 4
EXPERTS_PER_GROUP = 8
N_EXPERTS = N_GROUPS * EXPERTS_PER_GROUP
D_EXPERT = 512
NORM_EPS = 1e-6

LANES = 128
N_PAIRS = D_A // LANES
VMEM_LIMIT = 48 * 1024 * 1024

ROWS_MIX = 2
TM_PROJ = 256
TM_ROUTE = 512
TM_COMB = 256
BM_MOE = 256
ROUTE_LANES = LANES
DMA_UNROLL = 8


def _dot(a, b):
    return jnp.dot(a, b, preferred_element_type=F32)


def _dot_nt(a, b):
    return lax.dot_general(a, b, (((1,), (1,)), ((), ())), preferred_element_type=F32)


def _dot_tn(a, b):
    return lax.dot_general(a, b, (((0,), (0,)), ((), ())), preferred_element_type=F32)


def _split2(x):
    hi = x.astype(BF16)
    lo = (x - hi.astype(F32)).astype(BF16)
    return hi, lo


def _split3(x):
    x1 = x.astype(BF16)
    r1 = x - x1.astype(F32)
    x2 = r1.astype(BF16)
    x3 = (r1 - x2.astype(F32)).astype(BF16)
    return x1, x2, x3


def _dot_x2(x, w_exact):
    hi, lo = _split2(x)
    return _dot(hi, w_exact) + _dot(lo, w_exact)


def _dot_3pass(x, w_hi, w_lo):
    hi, lo = _split2(x)
    return _dot(hi, w_hi) + _dot(lo, w_hi) + _dot(hi, w_lo)


def _rms(x, g):
    return x * lax.rsqrt(jnp.mean(x * x, axis=-1, keepdims=True) + NORM_EPS) * g


def _in_proj_kernel(x_ref, g_ref, w_ref, z_ref):
    u = _rms(x_ref[...], g_ref[...])
    z_ref[...] = _dot(u.astype(BF16), w_ref[...])


def _in_proj(x2d, g, w_bf16):
    m, d = x2d.shape
    tm = min(TM_PROJ, m)
    n = w_bf16.shape[1]
    return pl.pallas_call(
        _in_proj_kernel,
        grid=(m // tm,),
        in_specs=[
            pl.BlockSpec((tm, d), lambda i: (i, 0)),
            pl.BlockSpec((1, d), lambda i: (0, 0)),
            pl.BlockSpec((d, n), lambda i: (0, 0)),
        ],
        out_specs=pl.BlockSpec((tm, n), lambda i: (i, 0)),
        out_shape=jax.ShapeDtypeStruct((m, n), F32),
        compiler_params=pltpu.CompilerParams(
            dimension_semantics=("arbitrary",), vmem_limit_bytes=VMEM_LIMIT),
        name="in_proj",
    )(x2d, g, w_bf16)


def _mixer_kernel(z_ref, zprev0_ref, sa0_ref, rb0_ref, cos_ref, sin_ref, mu_ref, pvec_ref,
                  wwa_hi_ref, wwa_lo_ref, wg_hi_ref, wg_lo_ref, bd_ref,
                  mixed_ref, sa_out_ref, rb_out_ref,
                  sa_scr, rb_scr, zlast_scr, *, rows):
    c_idx = pl.program_id(1)
    n_chunks = pl.num_programs(1)
    C = CHUNK
    R = rows
    M = R * C
    rws = [slice(rr * C, (rr + 1) * C) for rr in range(R)]

    @pl.when(c_idx == 0)
    def _():
        for rr in range(R):
            sa_scr[rr] = sa0_ref[...]
            rb_scr[rr] = rb0_ref[...]
            zlast_scr[rr] = zprev0_ref[...]

    z = jnp.concatenate([z_ref[rr] for rr in range(R)], axis=0)
    za = z[:, :P_A]
    row = lax.broadcasted_iota(jnp.int32, (M, 1), 0)
    zprev = pltpu.roll(za, 1, 0)
    for rr in range(R):
        zprev = jnp.where(row == rr * C, zlast_scr[rr], zprev)
        zlast_scr[rr] = za[(rr + 1) * C - 1:(rr + 1) * C, :]
    zs = za + (zprev - za) * mu_ref[...]

    r = zs[:, 0:D_A]
    k = zs[:, D_A:2 * D_A]
    v = zs[:, 2 * D_A:3 * D_A]
    wa = zs[:, 3 * D_A:3 * D_A + LORA_W + LORA_A]
    g_lo = zs[:, 3 * D_A + LORA_W + LORA_A:P_A]

    decay_w0 = pvec_ref[0:1, :]
    iclr_a0 = pvec_ref[1:2, :]
    k_k = pvec_ref[2:3, :]
    k_a = pvec_ref[3:4, :]
    r_k = pvec_ref[4:5, :]
    ln_w = pvec_ref[5:6, :]
    ln_b = pvec_ref[6:7, :]
    gn_w = pvec_ref[7:8, :]

    lane = lax.broadcasted_iota(jnp.int32, (1, LANES), 1)
    wa_act = jnp.where(lane < LORA_W, jnp.tanh(wa), wa)
    pre = _dot_3pass(wa_act, wwa_hi_ref[...], wwa_lo_ref[...])
    log_w = -DECAY_SCALE * jax.nn.sigmoid(decay_w0 + pre[:, :D_A])
    a = jax.nn.sigmoid(iclr_a0 + pre[:, D_A:])
    g = _dot_3pass(jax.nn.sigmoid(g_lo), wg_hi_ref[...], wg_lo_ref[...])

    bd = bd_ref[...]
    kk = k * k_k
    kk = kk / jnp.maximum(jnp.sqrt(_dot_x2(kk * kk, bd)), 1e-12)
    k2 = k * (1.0 + (a - 1.0) * k_a)
    b = kk * a

    tm_i = lax.broadcasted_iota(jnp.int32, (M, M), 0)
    sm_j = lax.broadcasted_iota(jnp.int32, (M, M), 1)
    same_row = lax.shift_right_logical(tm_i, LOG2_CHUNK) == lax.shift_right_logical(sm_j, LOG2_CHUNK)
    tril = ((sm_j <= tm_i) & same_row).astype(BF16)
    lw1, lw2, lw3 = _split3(log_w)
    cum = _dot(tril, lw1) + _dot(tril, lw2) + _dot(tril, lw3)
    cmid_r = [cum[rr * C + C // 2 - 1:rr * C + C // 2, :] for rr in range(R)]
    cc = cum - jnp.concatenate([jnp.broadcast_to(cm, (C, D_A)) for cm in cmid_r], axis=0)
    e_nc = jnp.exp(-cc)
    rg = r * jnp.exp(cc)
    kkg = kk * jnp.exp(cc - log_w)
    kinv = k2 * e_nc
    binv = b * e_nc
    e_mid = [jnp.exp(cm) for cm in cmid_r]
    e_end = [jnp.exp(cc[(rr + 1) * C - 1:(rr + 1) * C, :]) for rr in range(R)]
    gam = [jnp.exp(cum[(rr + 1) * C - 1:(rr + 1) * C, :]) for rr in range(R)]

    i2 = lax.broadcasted_iota(jnp.int32, (2 * C, 2 * C), 0)
    j2 = lax.broadcasted_iota(jnp.int32, (2 * C, 2 * C), 1)
    bi = i2 >= C
    bj = j2 >= C
    t2 = jnp.where(bi, i2 - C, i2)
    s2 = jnp.where(bj, j2 - C, j2)
    same_blk = bi == bj
    strict_same = (s2 < t2) & same_blk
    strict_cross = (s2 < t2) & jnp.logical_not(same_blk)
    ta = lax.broadcasted_iota(jnp.int32, (C, 4 * C), 0)
    ja = lax.broadcasted_iota(jnp.int32, (C, 4 * C), 1)
    incl = (ja & (C - 1)) <= ta
    m0 = lane < HEAD_A
    m1 = jnp.logical_not(m0)

    items = [(rr, p) for rr in range(R) for p in range(N_PAIRS)]
    n_it = range(len(items))
    rsl = [rws[rr] for rr, _ in items]
    lsl = [slice(p * LANES, (p + 1) * LANES) for _, p in items]
    kkg0 = [jnp.where(m0, kkg[rsl[i], lsl[i]], 0.0) for i in n_it]
    kkg1 = [jnp.where(m1, kkg[rsl[i], lsl[i]], 0.0) for i in n_it]
    kinv_i = [kinv[rsl[i], lsl[i]] for i in n_it]
    binv_i = [binv[rsl[i], lsl[i]] for i in n_it]
    v_i = [v[rsl[i], lsl[i]] for i in n_it]
    lhs = [jnp.concatenate([kkg[rsl[i], lsl[i]], rg[rsl[i], lsl[i]]], axis=0).astype(BF16) for i in n_it]
    rhs = [jnp.concatenate([jnp.where(m0, binv_i[i], 0.0), jnp.where(m0, kinv_i[i], 0.0),
                            jnp.where(m1, kinv_i[i], 0.0), jnp.where(m1, binv_i[i], 0.0)], axis=0).astype(BF16)
           for i in n_it]
    out = [_dot_nt(lhs[i], rhs[i]) for i in n_it]
    top = [jnp.concatenate([out[i][:C, :2 * C], out[i][:C, 2 * C:]], axis=0) for i in n_it]
    npow = [jnp.where(strict_same, -top[i], 0.0) for i in n_it]
    q_anti = [jnp.where(strict_cross, top[i], 0.0) for i in n_it]
    vm0 = [jnp.where(m0, v_i[i], 0.0) for i in n_it]
    vm1 = [jnp.where(m1, v_i[i], 0.0) for i in n_it]
    qv = [_dot(q_anti[i].astype(BF16), jnp.concatenate([vm1[i], vm0[i]], axis=0).astype(BF16)) for i in n_it]
    x = [jnp.concatenate([qv[i], jnp.concatenate([kkg0[i], kkg1[i]], axis=0)], axis=1) for i in n_it]
    for it in range(6):
        nb = [npow[i].astype(BF16) for i in n_it]
        x = [x[i] + _dot(nb[i], x[i].astype(BF16)) for i in n_it]
        if it < 5:
            npow = [_dot(nb[i], nb[i]) for i in n_it]
    s_old = [sa_scr[rr, p] for rr, p in items]
    s0m = [(s_old[i] * e_mid[items[i][0]][:, lsl[i]]).astype(BF16) for i in n_it]
    u_stack = [_dot_nt(x[i][:, LANES:].astype(BF16), s0m[i]) + x[i][:, :LANES] for i in n_it]
    y_it = [_dot_nt(rg[rsl[i], lsl[i]].astype(BF16), s0m[i]) for i in n_it]
    a_cat = [jnp.where(incl, out[i][C:], 0.0).astype(BF16) for i in n_it]
    y_it = [y_it[i] + _dot(a_cat[i], jnp.concatenate([-u_stack[i][:C], vm0[i], vm1[i], -u_stack[i][C:]],
                                                     axis=0).astype(BF16)) for i in n_it]
    gt = [_dot_tn(jnp.concatenate([v_i[i], -(u_stack[i][:C] + u_stack[i][C:])], axis=0).astype(BF16),
                  jnp.concatenate([kinv_i[i], binv_i[i]], axis=0).astype(BF16)) for i in n_it]
    for i in n_it:
        rr, p = items[i]
        sa_scr[rr, p] = s_old[i] * gam[rr][:, lsl[i]] + jnp.where(same_blk, gt[i], 0.0) * e_end[rr][:, lsl[i]]

    y = jnp.concatenate([jnp.concatenate(y_it[rr * N_PAIRS:(rr + 1) * N_PAIRS], axis=1) for rr in range(R)],
                        axis=0)
    inv_n = 1.0 / HEAD_A
    mean = _dot(y.astype(BF16), bd) * inv_n
    dlt = y - mean
    var = _dot((dlt * dlt).astype(BF16), bd) * inv_n
    yn = dlt * lax.rsqrt(var + GN_EPS_A)
    bonus = _dot((r * k2 * r_k).astype(BF16), bd) * v
    out_a = (yn * ln_w + ln_b + bonus) * g

    zb = z[:, P_A:]
    cosf = cos_ref[...]
    sinf = sin_ref[...]
    ti = lax.broadcasted_iota(jnp.int32, (C, C), 0)
    sj = lax.broadcasted_iota(jnp.int32, (C, C), 1)
    relf = (ti - sj).astype(F32)
    causal = sj <= ti
    rowf = lax.broadcasted_iota(jnp.int32, (C, 1), 0).astype(F32)
    hb = range(H_B)
    lgs = [math.log1p(-(2.0 ** (-5.0 - h))) for h in hb]
    decay_in = [jnp.where(causal, jnp.exp(lg * jnp.maximum(relf, 0.0)), 0.0) for lg in lgs]
    q_dec = [jnp.exp(lg * (rowf + 1.0)) for lg in lgs]
    k_dec = [jnp.exp(lg * (float(C - 1) - rowf)) for lg in lgs]
    bitems = [(rr, h) for rr in range(R) for h in hb]
    n_b = range(len(bitems))

    def zcol(i, part):
        rr, h = bitems[i]
        return zb[rws[rr], part * D_B + h * HEAD_B:part * D_B + (h + 1) * HEAD_B]

    def rope(t):
        return t * cosf + pltpu.roll(t, HEAD_B // 2, 1) * sinf

    q_b = [rope(zcol(i, 0)).astype(BF16) for i in n_b]
    k_r = [rope(zcol(i, 1)) * (HEAD_B ** -0.5) for i in n_b]
    v_b = [zcol(i, 2).astype(BF16) for i in n_b]
    scores = [_dot_nt(q_b[i], k_r[i].astype(BF16)) * decay_in[bitems[i][1]] for i in n_b]
    r_old = [rb_scr[rr, h] for rr, h in bitems]
    cross = [_dot(q_b[i], r_old[i].astype(BF16)) * q_dec[bitems[i][1]] for i in n_b]
    inner = [_dot(scores[i].astype(BF16), v_b[i]) for i in n_b]
    kv = [_dot_tn((k_r[i] * k_dec[bitems[i][1]]).astype(BF16), v_b[i]) for i in n_b]
    outs_b = []
    for i in n_b:
        rr, h = bitems[i]
        rb_scr[rr, h] = r_old[i] * math.exp(lgs[h] * C) + kv[i]
        y_h = inner[i] + cross[i]
        mu_h = jnp.mean(y_h, axis=-1, keepdims=True)
        d_h = y_h - mu_h
        var_h = jnp.mean(d_h * d_h, axis=-1, keepdims=True)
        yn_h = d_h * lax.rsqrt(var_h + GN_EPS_B)
        g_h = zcol(i, 3)
        outs_b.append(yn_h * gn_w[:, h * HEAD_B:(h + 1) * HEAD_B] * (g_h * jax.nn.sigmoid(g_h)))
    out_b = jnp.concatenate([jnp.concatenate(outs_b[rr * H_B:(rr + 1) * H_B], axis=1) for rr in range(R)], axis=0)

    mixed = jnp.concatenate([out_a, out_b], axis=1).astype(mixed_ref.dtype)
    for rr in range(R):
        mixed_ref[rr] = mixed[rws[rr]]

    @pl.when(c_idx == n_chunks - 1)
    def _():
        sa_out_ref[...] = sa_scr[...]
        rb_out_ref[...] = rb_scr[...]


def _mixer(z3, zprev0, sa0, rb0, cosf, sinf, mu, pvec, wwa_hi, wwa_lo, wg_hi, wg_lo, bd):
    bsz, length, _ = z3.shape
    n_chunks = length // CHUNK
    rows = ROWS_MIX if bsz % ROWS_MIX == 0 else 1
    const2 = lambda b, c: (0, 0)
    const3 = lambda b, c: (0, 0, 0)
    st_shape = (N_PAIRS, LANES, LANES)
    return pl.pallas_call(
        functools.partial(_mixer_kernel, rows=rows),
        grid=(bsz // rows, n_chunks),
        in_specs=[
            pl.BlockSpec((rows, CHUNK, P_IN), lambda b, c: (b, c, 0)),
            pl.BlockSpec((1, P_A), const2),
            pl.BlockSpec(st_shape, const3),
            pl.BlockSpec((H_B, HEAD_B, HEAD_B), const3),
            pl.BlockSpec((CHUNK, HEAD_B), lambda b, c: (c, 0)),
            pl.BlockSpec((CHUNK, HEAD_B), lambda b, c: (c, 0)),
            pl.BlockSpec((1, P_A), const2),
            pl.BlockSpec((8, D_A), const2),
            pl.BlockSpec((LORA_W + LORA_A, 2 * D_A), const2),
            pl.BlockSpec((LORA_W + LORA_A, 2 * D_A), const2),
            pl.BlockSpec((LORA_G, D_A), const2),
            pl.BlockSpec((LORA_G, D_A), const2),
            pl.BlockSpec((D_A, D_A), const2),
        ],
        out_specs=[
            pl.BlockSpec((rows, CHUNK, D_A + D_B), lambda b, c: (b, c, 0)),
            pl.BlockSpec((rows,) + st_shape, lambda b, c: (b, 0, 0, 0)),
            pl.BlockSpec((rows, H_B, HEAD_B, HEAD_B), lambda b, c: (b, 0, 0, 0)),
        ],
        out_shape=[
            jax.ShapeDtypeStruct((bsz, length, D_A + D_B), BF16),
            jax.ShapeDtypeStruct((bsz,) + st_shape, F32),
            jax.ShapeDtypeStruct((bsz, H_B, HEAD_B, HEAD_B), F32),
        ],
        scratch_shapes=[
            pltpu.VMEM((rows,) + st_shape, F32),
            pltpu.VMEM((rows, H_B, HEAD_B, HEAD_B), F32),
            pltpu.VMEM((rows, 1, P_A), F32),
        ],
        compiler_params=pltpu.CompilerParams(
            dimension_semantics=("arbitrary", "arbitrary"), vmem_limit_bytes=VMEM_LIMIT),
        name="mixer",
    )(z3, zprev0, sa0, rb0, cosf, sinf, mu, pvec, wwa_hi, wwa_lo, wg_hi, wg_lo, bd)


def _out_router_kernel(mixed_ref, x_ref, wout_ref, g_ref, wr_hi_ref, wr_lo_ref, br_ref, ltri_ref,
                       h1_ref, u2_ref, route_ref, counts_ref, cnt_scr):
    i = pl.program_id(0)

    @pl.when(i == 0)
    def _():
        cnt_scr[...] = jnp.zeros_like(cnt_scr)

    h1 = x_ref[...] + _dot(mixed_ref[...], wout_ref[...])
    h1_ref[...] = h1
    u2 = _rms(h1, g_ref[...])
    u2_ref[...] = u2
    logits = _dot_3pass(u2, wr_hi_ref[...], wr_lo_ref[...]) + br_ref[...]

    tm = logits.shape[0]
    lane = lax.broadcasted_iota(jnp.int32, (tm, ROUTE_LANES), 1).astype(F32)
    neg = -jnp.inf
    big = float(ROUTE_LANES)
    first = float(N_GROUPS)

    def rmax(t):
        return jnp.max(t, axis=-1, keepdims=True)

    def rsum(t):
        return jnp.sum(t, axis=-1, keepdims=True)

    def rmin(t):
        return jnp.min(t, axis=-1, keepdims=True)

    gmask = lane < N_GROUPS
    gmax = rmax(jnp.where(gmask, logits, neg))
    gexp = jnp.where(gmask, jnp.exp(logits - gmax), 0.0)
    gprob = gexp / rsum(gexp)
    g_p = rmax(gprob)
    g_idx = rmin(jnp.where(gmask & (gprob == g_p), lane, big))

    lo_lane = first + EXPERTS_PER_GROUP * g_idx
    emask = (lane >= lo_lane) & (lane < lo_lane + EXPERTS_PER_GROUP)
    emax = rmax(jnp.where(emask, logits, neg))
    eexp = jnp.where(emask, jnp.exp(logits - emax), 0.0)
    eprob = jnp.where(emask, eexp / rsum(eexp), -1.0)
    p1 = rmax(eprob)
    i1 = rmin(jnp.where(eprob == p1, lane, big))
    eprob2 = jnp.where(lane == i1, -1.0, eprob)
    p2 = rmax(eprob2)
    i2 = rmin(jnp.where(eprob2 == p2, lane, big))
    w1 = g_p * p1 / (p1 + p2)
    w2 = g_p * p2 / (p1 + p2)

    sel1 = lane == i1
    sel2 = lane == i2
    onehot = jnp.where(sel1 | sel2, 1.0, 0.0)
    before = _dot(ltri_ref[...], onehot.astype(BF16)) + cnt_scr[...]
    rank1 = rsum(jnp.where(sel1, before, 0.0))
    rank2 = rsum(jnp.where(sel2, before, 0.0))
    cnt_scr[...] = cnt_scr[...] + jnp.sum(onehot, axis=0, keepdims=True)
    counts_ref[...] = cnt_scr[...]

    rec = jnp.where(lane == 0, i1 - first, 0.0)
    rec = jnp.where(lane == 1, i2 - first, rec)
    rec = jnp.where(lane == 2, w1, rec)
    rec = jnp.where(lane == 3, w2, rec)
    rec = jnp.where(lane == 4, rank1, rec)
    rec = jnp.where(lane == 5, rank2, rec)
    route_ref[...] = rec


def _out_router(mixed2d, x2d, wout_bf16, g, wr_hi, wr_lo, br, ltri):
    t, d = x2d.shape
    tm = ltri.shape[0]
    c2 = lambda i: (0, 0)
    return pl.pallas_call(
        _out_router_kernel,
        grid=(t // tm,),
        in_specs=[
            pl.BlockSpec((tm, d), lambda i: (i, 0)),
            pl.BlockSpec((tm, d), lambda i: (i, 0)),
            pl.BlockSpec((d, d), c2),
            pl.BlockSpec((1, d), c2),
            pl.BlockSpec((d, ROUTE_LANES), c2),
            pl.BlockSpec((d, ROUTE_LANES), c2),
            pl.BlockSpec((1, ROUTE_LANES), c2),
            pl.BlockSpec((tm, tm), c2),
        ],
        out_specs=[
            pl.BlockSpec((tm, d), lambda i: (i, 0)),
            pl.BlockSpec((tm, d), lambda i: (i, 0)),
            pl.BlockSpec((tm, ROUTE_LANES), lambda i: (i, 0)),
            pl.BlockSpec((1, ROUTE_LANES), c2),
        ],
        out_shape=[
            jax.ShapeDtypeStruct((t, d), F32),
            jax.ShapeDtypeStruct((t, d), F32),
            jax.ShapeDtypeStruct((t, ROUTE_LANES), F32),
            jax.ShapeDtypeStruct((1, ROUTE_LANES), F32),
        ],
        scratch_shapes=[pltpu.VMEM((1, ROUTE_LANES), F32)],
        compiler_params=pltpu.CompilerParams(
            dimension_semantics=("arbitrary",), vmem_limit_bytes=VMEM_LIMIT),
        name="out_router",
    )(mixed2d, x2d, wout_bf16, g, wr_hi, wr_lo, br, ltri)


def _row_copy(src_ref, src_row, dst_ref, dst_row, sem):
    return pltpu.make_async_copy(src_ref.at[pl.ds(src_row, 1), :], dst_ref.at[pl.ds(dst_row, 1), :], sem)


def _dispatch_kernel(pad_end_ref, padded_ref, dest_ref, u2_ref, xrows_ref, zero_scr, zsem, sem):
    i = pl.program_id(0)
    tm = u2_ref.shape[0]
    n_blocks = xrows_ref.shape[0] // BM_MOE
    n_used = pad_end_ref[N_EXPERTS - 1] // BM_MOE

    def zero_block(start):
        return pltpu.make_async_copy(zero_scr, xrows_ref.at[pl.ds(pl.multiple_of(start, BM_MOE), BM_MOE), :], zsem)

    def zero_copy(e):
        return zero_block(pad_end_ref[e] - BM_MOE)

    @pl.when(i == 0)
    def _():
        zero_scr[...] = jnp.zeros_like(zero_scr)
        for e in range(N_EXPERTS):
            @pl.when(padded_ref[e] > 0)
            def _():
                zero_copy(e).start()

        def tail_start(j, carry):
            zero_block(j * BM_MOE).start()
            return carry

        lax.fori_loop(n_used, n_blocks, tail_start, 0)
        for e in range(N_EXPERTS):
            @pl.when(padded_ref[e] > 0)
            def _():
                zero_copy(e).wait()

        def tail_wait(j, carry):
            zero_block(j * BM_MOE).wait()
            return carry

        lax.fori_loop(n_used, n_blocks, tail_wait, 0)

    def issue(j, carry):
        _row_copy(u2_ref, j, xrows_ref, dest_ref[0, 0, 2 * j], sem).start()
        _row_copy(u2_ref, j, xrows_ref, dest_ref[0, 0, 2 * j + 1], sem).start()
        return carry

    lax.fori_loop(0, tm, issue, 0, unroll=DMA_UNROLL)

    def drain(j, carry):
        _row_copy(u2_ref, j, xrows_ref, dest_ref[0, 0, 2 * j], sem).wait()
        _row_copy(u2_ref, j, xrows_ref, dest_ref[0, 0, 2 * j + 1], sem).wait()
        return carry

    lax.fori_loop(0, tm, drain, 0, unroll=DMA_UNROLL)


def _dispatch(pad_end, padded, dest3, u2, n_rows):
    t, d = u2.shape
    tm = dest3.shape[2] // 2
    grid_spec = pltpu.PrefetchScalarGridSpec(
        num_scalar_prefetch=2,
        grid=(t // tm,),
        in_specs=[
            pl.BlockSpec((1, 1, 2 * tm), lambda i, pe, pd: (i, 0, 0), memory_space=pltpu.SMEM),
            pl.BlockSpec((tm, d), lambda i, pe, pd: (i, 0)),
        ],
        out_specs=pl.BlockSpec(memory_space=pl.ANY),
        scratch_shapes=[
            pltpu.VMEM((BM_MOE, d), F32),
            pltpu.SemaphoreType.DMA(()),
            pltpu.SemaphoreType.DMA(()),
        ],
    )
    return pl.pallas_call(
        _dispatch_kernel,
        grid_spec=grid_spec,
        out_shape=jax.ShapeDtypeStruct((n_rows, d), F32),
        compiler_params=pltpu.CompilerParams(
            dimension_semantics=("arbitrary",), vmem_limit_bytes=VMEM_LIMIT),
        name="dispatch",
    )(pad_end, padded, dest3, u2)


def _expert_ffn_kernel(be_ref, nused_ref, x_ref, wg_ref, wu_ref, wd_ref, y_ref):
    i = pl.program_id(0)

    @pl.when(i < nused_ref[0])
    def _():
        x = x_ref[...].astype(BF16)
        hg = _dot(x, wg_ref[0])
        hu = _dot(x, wu_ref[0])
        hid = (hg * jax.nn.sigmoid(hg)) * hu
        y_ref[...] = _dot(hid.astype(BF16), wd_ref[0])

    @pl.when(i >= nused_ref[0])
    def _():
        y_ref[...] = jnp.zeros_like(y_ref)


def _expert_ffn(block_expert, n_used, x_rows, wg, wu, wd):
    n_rows, d = x_rows.shape
    n_blocks = n_rows // BM_MOE
    de = wg.shape[2]

    def row_map(i, be, nu):
        return (jnp.minimum(i, nu[0] - 1), 0)

    def w_map(i, be, nu):
        return (be[jnp.minimum(i, nu[0] - 1)], 0, 0)

    grid_spec = pltpu.PrefetchScalarGridSpec(
        num_scalar_prefetch=2,
        grid=(n_blocks,),
        in_specs=[
            pl.BlockSpec((BM_MOE, d), row_map),
            pl.BlockSpec((1, d, de), w_map),
            pl.BlockSpec((1, d, de), w_map),
            pl.BlockSpec((1, de, d), w_map),
        ],
        out_specs=pl.BlockSpec((BM_MOE, d), lambda i, be, nu: (i, 0)),
    )
    return pl.pallas_call(
        _expert_ffn_kernel,
        grid_spec=grid_spec,
        out_shape=jax.ShapeDtypeStruct((n_rows, d), F32),
        compiler_params=pltpu.CompilerParams(
            dimension_semantics=("arbitrary",), vmem_limit_bytes=VMEM_LIMIT),
        name="expert_ffn",
    )(block_expert, n_used, x_rows, wg, wu, wd)


def _combine_kernel(dest_ref, h1_ref, route_ref, g_ref, yrows_ref, out_ref, ybuf, sem):
    tm = h1_ref.shape[0]

    def copies(j):
        return (_row_copy(yrows_ref, dest_ref[0, 0, 2 * j], ybuf.at[0], j, sem),
                _row_copy(yrows_ref, dest_ref[0, 0, 2 * j + 1], ybuf.at[1], j, sem))

    def issue(j, carry):
        c0, c1 = copies(j)
        c0.start()
        c1.start()
        return carry

    lax.fori_loop(0, tm, issue, 0, unroll=DMA_UNROLL)

    def drain(j, carry):
        c0, c1 = copies(j)
        c0.wait()
        c1.wait()
        return carry

    lax.fori_loop(0, tm, drain, 0, unroll=DMA_UNROLL)

    rec = route_ref[...]
    w1 = rec[:, 2:3]
    w2 = rec[:, 3:4]
    h = h1_ref[...] + ybuf[0] * w1 + ybuf[1] * w2
    out_ref[...] = _rms(h, g_ref[...])


def _combine(dest3, h1, route, g, y_rows):
    t, d = h1.shape
    tm = dest3.shape[2] // 2
    return pl.pallas_call(
        _combine_kernel,
        grid=(t // tm,),
        in_specs=[
            pl.BlockSpec((1, 1, 2 * tm), lambda i: (i, 0, 0), memory_space=pltpu.SMEM),
            pl.BlockSpec((tm, d), lambda i: (i, 0)),
            pl.BlockSpec((tm, ROUTE_LANES), lambda i: (i, 0)),
            pl.BlockSpec((1, d), lambda i: (0, 0)),
            pl.BlockSpec(memory_space=pl.ANY),
        ],
        out_specs=pl.BlockSpec((tm, d), lambda i: (i, 0)),
        out_shape=jax.ShapeDtypeStruct((t, d), F32),
        scratch_shapes=[pltpu.VMEM((2, tm, d), F32), pltpu.SemaphoreType.DMA(())],
        compiler_params=pltpu.CompilerParams(
            dimension_semantics=("arbitrary",), vmem_limit_bytes=VMEM_LIMIT),
        name="combine",
    )(dest3, h1, route, g, y_rows)


def _rope_tables(positions):
    inv_freq = ROPE_BASE ** (-jnp.arange(0, HEAD_B, 2, dtype=F32) / HEAD_B)
    ang = positions.astype(F32)[:, None] * inv_freq[None, :]
    cos, sin = jnp.cos(ang), jnp.sin(ang)
    return jnp.concatenate([cos, cos], axis=1), jnp.concatenate([-sin, sin], axis=1)


def _hi_lo(w):
    hi = w.astype(BF16)
    return hi, (w - hi.astype(F32)).astype(BF16)


def kernel(x, meta_tokens, norm_mix, w_in, shift_mu, decay_w0, decay_up, iclr_a0, iclr_up, gate_up, k_k, k_a, r_k, ln_w_a, ln_b_a, gn_w_b, w_out, norm_ffn, router_group_w, router_group_b, router_expert_w, router_expert_b, moe_w_gate, moe_w_up, moe_w_down, norm_final):
    bsz, seq, d = x.shape
    assert d == D_MODEL and seq % CHUNK == 0 and norm_mix.shape[0] == 1
    t = bsz * seq
    li = 0

    w_in_b = w_in[li].astype(BF16)
    g_mix = norm_mix[li][None, :]
    mu = shift_mu[li][None, :]
    pvec = jnp.stack([decay_w0[li], iclr_a0[li], k_k[li], k_a[li], r_k[li], ln_w_a[li], ln_b_a[li], gn_w_b[li]])
    wwa = jnp.zeros((LORA_W + LORA_A, 2 * D_A), F32)
    wwa = wwa.at[:LORA_W, :D_A].set(decay_up[li]).at[LORA_W:, D_A:].set(iclr_up[li])
    wwa_hi, wwa_lo = _hi_lo(wwa)
    wg_hi, wg_lo = _hi_lo(gate_up[li])
    ch = jnp.arange(D_A) // HEAD_A
    bd = (ch[:, None] == ch[None, :]).astype(BF16)

    meta_pad = jnp.concatenate([jnp.zeros((CHUNK - N_META, d), F32), meta_tokens.astype(F32)], axis=0)
    z_meta = _in_proj(meta_pad, g_mix, w_in_b)
    cos_m, sin_m = _rope_tables(jnp.arange(CHUNK) - (CHUNK - N_META))
    zeros_a = jnp.zeros((N_PAIRS, LANES, LANES), F32)
    zeros_b = jnp.zeros((H_B, HEAD_B, HEAD_B), F32)
    _, sa_meta, rb_meta = _mixer(z_meta[None], jnp.zeros((1, P_A), F32), zeros_a, zeros_b, cos_m, sin_m,
                                 mu, pvec, wwa_hi, wwa_lo, wg_hi, wg_lo, bd)

    x2d = x.reshape(t, d)
    z = _in_proj(x2d, g_mix, w_in_b)
    cos_x, sin_x = _rope_tables(N_META + jnp.arange(seq))
    mixed, _, _ = _mixer(z.reshape(bsz, seq, P_IN), z_meta[CHUNK - 1:CHUNK, :P_A], sa_meta[0], rb_meta[0],
                         cos_x, sin_x, mu, pvec, wwa_hi, wwa_lo, wg_hi, wg_lo, bd)

    tm_r = min(TM_ROUTE, t)
    wr = jnp.zeros((d, ROUTE_LANES), F32)
    wr = wr.at[:, :N_GROUPS].set(router_group_w[li]).at[:, N_GROUPS:N_GROUPS + N_EXPERTS].set(router_expert_w[li])
    wr_hi, wr_lo = _hi_lo(wr)
    br = jnp.zeros((1, ROUTE_LANES), F32)
    br = br.at[0, :N_GROUPS].set(router_group_b[li]).at[0, N_GROUPS:N_GROUPS + N_EXPERTS].set(router_expert_b[li])
    ii = jnp.arange(tm_r)
    ltri = (ii[None, :] < ii[:, None]).astype(BF16)
    h1, u2, route, counts = _out_router(mixed.reshape(t, d), x2d, w_out[li].astype(BF16), norm_ffn[li][None, :],
                                        wr_hi, wr_lo, br, ltri)

    n_blocks = -(-(2 * t + N_EXPERTS * (BM_MOE - 1)) // BM_MOE)
    n_rows = n_blocks * BM_MOE
    cnt = counts[0, N_GROUPS:N_GROUPS + N_EXPERTS].astype(jnp.int32)
    padded = (cnt + BM_MOE - 1) // BM_MOE * BM_MOE
    pad_end = jnp.cumsum(padded)
    pad_start = pad_end - padded
    n_used = (pad_end[-1:] // BM_MOE).astype(jnp.int32)
    blk_start = jnp.arange(n_blocks, dtype=jnp.int32) * BM_MOE
    block_expert = jnp.minimum(jnp.sum(pad_end[None, :] <= blk_start[:, None], axis=1), N_EXPERTS - 1).astype(jnp.int32)
    eids = route[:, 0:2].astype(jnp.int32)
    ranks = route[:, 4:6].astype(jnp.int32)
    dest = pad_start[eids] + ranks

    x_rows = _dispatch(pad_end.astype(jnp.int32), padded.astype(jnp.int32),
                       dest.reshape(t // tm_r, 1, 2 * tm_r), u2, n_rows)
    y_rows = _expert_ffn(block_expert, n_used, x_rows, moe_w_gate[li].astype(BF16),
                         moe_w_up[li].astype(BF16), moe_w_down[li].astype(BF16))
    tm_c = min(TM_COMB, t)
    out = _combine(dest.reshape(t // tm_c, 1, 2 * tm_c), h1, route, norm_final[None, :], y_rows)
    return out.reshape(bsz, seq, d)
```

```python
import functools
import math

import jax
import jax.numpy as jnp
from jax import lax
from jax.experimental import pallas as pl
from jax.experimental.pallas import tpu as pltpu

F32 = jnp.float32
BF16 = jnp.bfloat16

D_MODEL = 1024
CHUNK = 64
LOG2_CHUNK = 6
N_META = 16
D_A = 512
HEAD_A = 64
LORA_W = 64
LORA_A = 64
LORA_G = 128
DECAY_SCALE = math.exp(-0.5)
GN_EPS_A = 64e-5
D_B = 512
H_B = 4
HEAD_B = 128
ROPE_BASE = 10000.0
GN_EPS_B = 1e-5
P_A = 3 * D_A + LORA_W + LORA_A + LORA_G
P_B = 4 * D_B
P_IN = P_A + P_B
N_GROUPS = 4
EXPERTS_PER_GROUP = 8
N_EXPERTS = N_GROUPS * EXPERTS_PER_GROUP
D_EXPERT = 512
NORM_EPS = 1e-6

LANES = 128
N_PAIRS = D_A // LANES
ROW_TILES = D_MODEL // LANES
HEAD_SUM_W = 256
VMEM_LIMIT = 48 * 1024 * 1024

ROWS_MIX = 2
TM_PROJ = 256
TM_ROUTE = 512
TM_COMB = 256
BM_MOE = 256
ROUTE_LANES = LANES
ROUTE_ROWS = 8
DMA_UNROLL = 8


def _dot(a, b):
    return jnp.dot(a, b, preferred_element_type=F32)


def _dot_nt(a, b):
    return lax.dot_general(a, b, (((1,), (1,)), ((), ())), preferred_element_type=F32)


def _dot_tn(a, b):
    return lax.dot_general(a, b, (((0,), (0,)), ((), ())), preferred_element_type=F32)


def _split2(x):
    hi = x.astype(BF16)
    lo = (x - hi.astype(F32)).astype(BF16)
    return hi, lo


def _split3(x):
    x1 = x.astype(BF16)
    r1 = x - x1.astype(F32)
    x2 = r1.astype(BF16)
    x3 = (r1 - x2.astype(F32)).astype(BF16)
    return x1, x2, x3


def _dot_x2(x, w_exact):
    hi, lo = _split2(x)
    return _dot(hi, w_exact) + _dot(lo, w_exact)


def _dot_3pass(x, w_hi, w_lo):
    hi, lo = _split2(x)
    return _dot(hi, w_hi) + _dot(lo, w_hi) + _dot(hi, w_lo)


def _rms(x, g):
    return x * lax.rsqrt(jnp.mean(x * x, axis=-1, keepdims=True) + NORM_EPS) * g


def _store_rows(ref, x):
    n = ref.shape[0] // ROW_TILES
    for c in range(ROW_TILES):
        ref[pl.ds(c, n, stride=ROW_TILES), :] = x[:, c * LANES:(c + 1) * LANES]


def _load_rows(ref):
    n = ref.shape[0] // ROW_TILES
    return jnp.concatenate([ref[pl.ds(c, n, stride=ROW_TILES), :] for c in range(ROW_TILES)], axis=1)


def _row_tile(ref, row):
    return ref.at[pl.ds(pl.multiple_of(row * ROW_TILES, ROW_TILES), ROW_TILES), :]


def _in_proj_kernel(x_ref, g_ref, w_ref, z_ref):
    u = _rms(x_ref[...], g_ref[...])
    z_ref[...] = _dot(u.astype(BF16), w_ref[...])


def _in_proj(x2d, g, w_bf16):
    m, d = x2d.shape
    tm = min(TM_PROJ, m)
    n = w_bf16.shape[1]
    return pl.pallas_call(
        _in_proj_kernel,
        grid=(m // tm,),
        in_specs=[
            pl.BlockSpec((tm, d), lambda i: (i, 0)),
            pl.BlockSpec((1, d), lambda i: (0, 0)),
            pl.BlockSpec((d, n), lambda i: (0, 0)),
        ],
        out_specs=pl.BlockSpec((tm, n), lambda i: (i, 0)),
        out_shape=jax.ShapeDtypeStruct((m, n), F32),
        compiler_params=pltpu.CompilerParams(
            dimension_semantics=("arbitrary",), vmem_limit_bytes=VMEM_LIMIT),
        name="in_proj",
    )(x2d, g, w_bf16)


def _mixer_kernel(z_ref, zprev0_ref, sa0_ref, rb0_ref, cos_ref, sin_ref, mu_ref, pvec_ref,
                  wwa_hi_ref, wwa_lo_ref, wg_hi_ref, wg_lo_ref, bd_ref,
                  mixed_ref, sa_out_ref, rb_out_ref,
                  sa_scr, rb_scr, zlast_scr, *, rows):
    c_idx = pl.program_id(1)
    n_chunks = pl.num_programs(1)
    C = CHUNK
    R = rows
    M = R * C
    rws = [slice(rr * C, (rr + 1) * C) for rr in range(R)]

    @pl.when(c_idx == 0)
    def _():
        for rr in range(R):
            sa_scr[rr] = sa0_ref[...]
            rb_scr[rr] = rb0_ref[...]
            zlast_scr[rr] = zprev0_ref[...]

    z = jnp.concatenate([z_ref[rr] for rr in range(R)], axis=0)
    za = z[:, :P_A]
    row = lax.broadcasted_iota(jnp.int32, (M, 1), 0)
    zprev = pltpu.roll(za, 1, 0)
    for rr in range(R):
        zprev = jnp.where(row == rr * C, zlast_scr[rr], zprev)
        zlast_scr[rr] = za[(rr + 1) * C - 1:(rr + 1) * C, :]
    zs = za + (zprev - za) * mu_ref[...]

    r = zs[:, 0:D_A]
    k = zs[:, D_A:2 * D_A]
    v = zs[:, 2 * D_A:3 * D_A]
    wa = zs[:, 3 * D_A:3 * D_A + LORA_W + LORA_A]
    g_lo = zs[:, 3 * D_A + LORA_W + LORA_A:P_A]

    decay_w0 = pvec_ref[0:1, :]
    iclr_a0 = pvec_ref[1:2, :]
    k_k = pvec_ref[2:3, :]
    k_a = pvec_ref[3:4, :]
    r_k = pvec_ref[4:5, :]
    ln_w = pvec_ref[5:6, :]
    ln_b = pvec_ref[6:7, :]
    gn_w = pvec_ref[7:8, :]

    lane = lax.broadcasted_iota(jnp.int32, (1, LANES), 1)
    wa_act = jnp.where(lane < LORA_W, jnp.tanh(wa), wa)
    pre = _dot_3pass(wa_act, wwa_hi_ref[...], wwa_lo_ref[...])
    log_w = -DECAY_SCALE * jax.nn.sigmoid(decay_w0 + pre[:, :D_A])
    a = jax.nn.sigmoid(iclr_a0 + pre[:, D_A:])
    g = _dot_3pass(jax.nn.sigmoid(g_lo), wg_hi_ref[...], wg_lo_ref[...])

    bd = bd_ref[...]

    def head_sum(t):
        return jnp.concatenate([_dot(t[:, j * HEAD_SUM_W:(j + 1) * HEAD_SUM_W], bd)
                                for j in range(D_A // HEAD_SUM_W)], axis=1)

    kk = k * k_k
    kk2_hi, kk2_lo = _split2(kk * kk)
    kk = kk / jnp.maximum(jnp.sqrt(head_sum(kk2_hi) + head_sum(kk2_lo)), 1e-12)
    k2 = k * (1.0 + (a - 1.0) * k_a)
    b = kk * a

    tm_i = lax.broadcasted_iota(jnp.int32, (M, M), 0)
    sm_j = lax.broadcasted_iota(jnp.int32, (M, M), 1)
    same_row = lax.shift_right_logical(tm_i, LOG2_CHUNK) == lax.shift_right_logical(sm_j, LOG2_CHUNK)
    tril = ((sm_j <= tm_i) & same_row).astype(BF16)
    lw1, lw2, lw3 = _split3(log_w)
    cum = _dot(tril, lw1) + _dot(tril, lw2) + _dot(tril, lw3)
    cmid_r = [cum[rr * C + C // 2 - 1:rr * C + C // 2, :] for rr in range(R)]
    cc = cum - jnp.concatenate([jnp.broadcast_to(cm, (C, D_A)) for cm in cmid_r], axis=0)
    e_nc = jnp.exp(-cc)
    rg = r * jnp.exp(cc)
    kkg = kk * jnp.exp(cc - log_w)
    kinv = k2 * e_nc
    binv = b * e_nc
    e_mid = [jnp.exp(cm) for cm in cmid_r]
    e_end = [jnp.exp(cc[(rr + 1) * C - 1:(rr + 1) * C, :]) for rr in range(R)]
    gam = [jnp.exp(cum[(rr + 1) * C - 1:(rr + 1) * C, :]) for rr in range(R)]

    i2 = lax.broadcasted_iota(jnp.int32, (2 * C, 2 * C), 0)
    j2 = lax.broadcasted_iota(jnp.int32, (2 * C, 2 * C), 1)
    bi = i2 >= C
    bj = j2 >= C
    t2 = jnp.where(bi, i2 - C, i2)
    s2 = jnp.where(bj, j2 - C, j2)
    same_blk = bi == bj
    strict_same = (s2 < t2) & same_blk
    strict_cross = (s2 < t2) & jnp.logical_not(same_blk)
    ta = lax.broadcasted_iota(jnp.int32, (C, 4 * C), 0)
    ja = lax.broadcasted_iota(jnp.int32, (C, 4 * C), 1)
    incl = (ja & (C - 1)) <= ta
    m0 = lane < HEAD_A
    m1 = jnp.logical_not(m0)

    items = [(rr, p) for rr in range(R) for p in range(N_PAIRS)]
    n_it = range(len(items))
    rsl = [rws[rr] for rr, _ in items]
    lsl = [slice(p * LANES, (p + 1) * LANES) for _, p in items]
    kkg0 = [jnp.where(m0, kkg[rsl[i], lsl[i]], 0.0) for i in n_it]
    kkg1 = [jnp.where(m1, kkg[rsl[i], lsl[i]], 0.0) for i in n_it]
    kinv_i = [kinv[rsl[i], lsl[i]] for i in n_it]
    binv_i = [binv[rsl[i], lsl[i]] for i in n_it]
    v_i = [v[rsl[i], lsl[i]] for i in n_it]
    lhs = [jnp.concatenate([kkg[rsl[i], lsl[i]], rg[rsl[i], lsl[i]]], axis=0).astype(BF16) for i in n_it]
    rhs = [jnp.concatenate([jnp.where(m0, binv_i[i], 0.0), jnp.where(m0, kinv_i[i], 0.0),
                            jnp.where(m1, kinv_i[i], 0.0), jnp.where(m1, binv_i[i], 0.0)], axis=0).astype(BF16)
           for i in n_it]
    out = [_dot_nt(lhs[i], rhs[i]) for i in n_it]
    top = [jnp.concatenate([out[i][:C, :2 * C], out[i][:C, 2 * C:]], axis=0) for i in n_it]
    npow = [jnp.where(strict_same, -top[i], 0.0) for i in n_it]
    q_anti = [jnp.where(strict_cross, top[i], 0.0) for i in n_it]
    vm0 = [jnp.where(m0, v_i[i], 0.0) for i in n_it]
    vm1 = [jnp.where(m1, v_i[i], 0.0) for i in n_it]
    qv = [_dot(q_anti[i].astype(BF16), jnp.concatenate([vm1[i], vm0[i]], axis=0).astype(BF16)) for i in n_it]
    x = [jnp.concatenate([qv[i], jnp.concatenate([kkg0[i], kkg1[i]], axis=0)], axis=1) for i in n_it]
    for it in range(6):
        nb = [npow[i].astype(BF16) for i in n_it]
        x = [x[i] + _dot(nb[i], x[i].astype(BF16)) for i in n_it]
        if it < 5:
            npow = [_dot(nb[i], nb[i]) for i in n_it]
    s_old = [sa_scr[rr, p] for rr, p in items]
    s0m = [(s_old[i] * e_mid[items[i][0]][:, lsl[i]]).astype(BF16) for i in n_it]
    u_stack = [_dot_nt(x[i][:, LANES:].astype(BF16), s0m[i]) + x[i][:, :LANES] for i in n_it]
    y_it = [_dot_nt(rg[rsl[i], lsl[i]].astype(BF16), s0m[i]) for i in n_it]
    a_cat = [jnp.where(incl, out[i][C:], 0.0).astype(BF16) for i in n_it]
    y_it = [y_it[i] + _dot(a_cat[i], jnp.concatenate([-u_stack[i][:C], vm0[i], vm1[i], -u_stack[i][C:]],
                                                     axis=0).astype(BF16)) for i in n_it]
    gt = [_dot_tn(jnp.concatenate([v_i[i], -(u_stack[i][:C] + u_stack[i][C:])], axis=0).astype(BF16),
                  jnp.concatenate([kinv_i[i], binv_i[i]], axis=0).astype(BF16)) for i in n_it]
    for i in n_it:
        rr, p = items[i]
        sa_scr[rr, p] = s_old[i] * gam[rr][:, lsl[i]] + jnp.where(same_blk, gt[i], 0.0) * e_end[rr][:, lsl[i]]

    y = jnp.concatenate([jnp.concatenate(y_it[rr * N_PAIRS:(rr + 1) * N_PAIRS], axis=1) for rr in range(R)],
                        axis=0)
    inv_n = 1.0 / HEAD_A
    mean = head_sum(y.astype(BF16)) * inv_n
    dlt = y - mean
    var = head_sum((dlt * dlt).astype(BF16)) * inv_n
    yn = dlt * lax.rsqrt(var + GN_EPS_A)
    bonus = head_sum((r * k2 * r_k).astype(BF16)) * v
    out_a = (yn * ln_w + ln_b + bonus) * g

    zb = z[:, P_A:]
    cosf = cos_ref[...]
    sinf = sin_ref[...]
    ti = lax.broadcasted_iota(jnp.int32, (C, C), 0)
    sj = lax.broadcasted_iota(jnp.int32, (C, C), 1)
    relf = (ti - sj).astype(F32)
    causal = sj <= ti
    rowf = lax.broadcasted_iota(jnp.int32, (C, 1), 0).astype(F32)
    hb = range(H_B)
    lgs = [math.log1p(-(2.0 ** (-5.0 - h))) for h in hb]
    decay_in = [jnp.where(causal, jnp.exp(lg * jnp.maximum(relf, 0.0)), 0.0) for lg in lgs]
    q_dec = [jnp.exp(lg * (rowf + 1.0)) for lg in lgs]
    k_dec = [jnp.exp(lg * (float(C - 1) - rowf)) for lg in lgs]
    bitems = [(rr, h) for rr in range(R) for h in hb]
    n_b = range(len(bitems))

    def zcol(i, part):
        rr, h = bitems[i]
        return zb[rws[rr], part * D_B + h * HEAD_B:part * D_B + (h + 1) * HEAD_B]

    def rope(t):
        return t * cosf + pltpu.roll(t, HEAD_B // 2, 1) * sinf

    q_b = [rope(zcol(i, 0)).astype(BF16) for i in n_b]
    k_r = [rope(zcol(i, 1)) * (HEAD_B ** -0.5) for i in n_b]
    v_b = [zcol(i, 2).astype(BF16) for i in n_b]
    scores = [_dot_nt(q_b[i], k_r[i].astype(BF16)) * decay_in[bitems[i][1]] for i in n_b]
    r_old = [rb_scr[rr, h] for rr, h in bitems]
    cross = [_dot(q_b[i], r_old[i].astype(BF16)) * q_dec[bitems[i][1]] for i in n_b]
    inner = [_dot(scores[i].astype(BF16), v_b[i]) for i in n_b]
    kv = [_dot_tn((k_r[i] * k_dec[bitems[i][1]]).astype(BF16), v_b[i]) for i in n_b]
    outs_b = []
    for i in n_b:
        rr, h = bitems[i]
        rb_scr[rr, h] = r_old[i] * math.exp(lgs[h] * C) + kv[i]
        y_h = inner[i] + cross[i]
        mu_h = jnp.mean(y_h, axis=-1, keepdims=True)
        d_h = y_h - mu_h
        var_h = jnp.mean(d_h * d_h, axis=-1, keepdims=True)
        yn_h = d_h * lax.rsqrt(var_h + GN_EPS_B)
        g_h = zcol(i, 3)
        outs_b.append(yn_h * gn_w[:, h * HEAD_B:(h + 1) * HEAD_B] * (g_h * jax.nn.sigmoid(g_h)))
    out_b = jnp.concatenate([jnp.concatenate(outs_b[rr * H_B:(rr + 1) * H_B], axis=1) for rr in range(R)], axis=0)

    mixed = jnp.concatenate([out_a, out_b], axis=1).astype(mixed_ref.dtype)
    for rr in range(R):
        mixed_ref[rr] = mixed[rws[rr]]

    @pl.when(c_idx == n_chunks - 1)
    def _():
        sa_out_ref[...] = sa_scr[...]
        rb_out_ref[...] = rb_scr[...]


def _mixer(z3, zprev0, sa0, rb0, cosf, sinf, mu, pvec, wwa_hi, wwa_lo, wg_hi, wg_lo, bd):
    bsz, length, _ = z3.shape
    n_chunks = length // CHUNK
    rows = ROWS_MIX if bsz % ROWS_MIX == 0 else 1
    const2 = lambda b, c: (0, 0)
    const3 = lambda b, c: (0, 0, 0)
    st_shape = (N_PAIRS, LANES, LANES)
    return pl.pallas_call(
        functools.partial(_mixer_kernel, rows=rows),
        grid=(bsz // rows, n_chunks),
        in_specs=[
            pl.BlockSpec((rows, CHUNK, P_IN), lambda b, c: (b, c, 0)),
            pl.BlockSpec((1, P_A), const2),
            pl.BlockSpec(st_shape, const3),
            pl.BlockSpec((H_B, HEAD_B, HEAD_B), const3),
            pl.BlockSpec((CHUNK, HEAD_B), lambda b, c: (c, 0)),
            pl.BlockSpec((CHUNK, HEAD_B), lambda b, c: (c, 0)),
            pl.BlockSpec((1, P_A), const2),
            pl.BlockSpec((8, D_A), const2),
            pl.BlockSpec((LORA_W + LORA_A, 2 * D_A), const2),
            pl.BlockSpec((LORA_W + LORA_A, 2 * D_A), const2),
            pl.BlockSpec((LORA_G, D_A), const2),
            pl.BlockSpec((LORA_G, D_A), const2),
            pl.BlockSpec((HEAD_SUM_W, HEAD_SUM_W), const2),
        ],
        out_specs=[
            pl.BlockSpec((rows, CHUNK, D_A + D_B), lambda b, c: (b, c, 0)),
            pl.BlockSpec((rows,) + st_shape, lambda b, c: (b, 0, 0, 0)),
            pl.BlockSpec((rows, H_B, HEAD_B, HEAD_B), lambda b, c: (b, 0, 0, 0)),
        ],
        out_shape=[
            jax.ShapeDtypeStruct((bsz, length, D_A + D_B), BF16),
            jax.ShapeDtypeStruct((bsz,) + st_shape, F32),
            jax.ShapeDtypeStruct((bsz, H_B, HEAD_B, HEAD_B), F32),
        ],
        scratch_shapes=[
            pltpu.VMEM((rows,) + st_shape, F32),
            pltpu.VMEM((rows, H_B, HEAD_B, HEAD_B), F32),
            pltpu.VMEM((rows, 1, P_A), F32),
        ],
        compiler_params=pltpu.CompilerParams(
            dimension_semantics=("arbitrary", "arbitrary"), vmem_limit_bytes=VMEM_LIMIT),
        name="mixer",
    )(z3, zprev0, sa0, rb0, cosf, sinf, mu, pvec, wwa_hi, wwa_lo, wg_hi, wg_lo, bd)


def _out_router_kernel(mixed_ref, x_ref, wout_ref, g_ref, wr_hi_ref, wr_lo_ref, br_ref, ltri_ref,
                       h1_ref, u2_ref, route_ref, route_t_ref, counts_ref, cnt_scr):
    i = pl.program_id(0)

    @pl.when(i == 0)
    def _():
        cnt_scr[...] = jnp.zeros_like(cnt_scr)

    h1 = x_ref[...] + _dot(mixed_ref[...], wout_ref[...])
    h1_ref[...] = h1
    u2 = _rms(h1, g_ref[...])
    _store_rows(u2_ref, u2)
    logits =_dot_3pass(u2, wr_hi_ref[...], wr_lo_ref[...]) + br_ref[...]

    tm = logits.shape[0]
    lane = lax.broadcasted_iota(jnp.int32, (tm, ROUTE_LANES), 1).astype(F32)
    neg = -jnp.inf
    big = float(ROUTE_LANES)
    first = float(N_GROUPS)

    def rmax(t):
        return jnp.max(t, axis=-1, keepdims=True)

    def rsum(t):
        return jnp.sum(t, axis=-1, keepdims=True)

    def rmin(t):
        return jnp.min(t, axis=-1, keepdims=True)

    gmask = lane < N_GROUPS
    gmax = rmax(jnp.where(gmask, logits, neg))
    gexp = jnp.where(gmask, jnp.exp(logits - gmax), 0.0)
    gprob = gexp / rsum(gexp)
    g_p = rmax(gprob)
    g_idx = rmin(jnp.where(gmask & (gprob == g_p), lane, big))

    lo_lane = first + EXPERTS_PER_GROUP * g_idx
    emask = (lane >= lo_lane) & (lane < lo_lane + EXPERTS_PER_GROUP)
    emax = rmax(jnp.where(emask, logits, neg))
    eexp = jnp.where(emask, jnp.exp(logits - emax), 0.0)
    eprob = jnp.where(emask, eexp / rsum(eexp), -1.0)
    p1 = rmax(eprob)
    i1 = rmin(jnp.where(eprob == p1, lane, big))
    eprob2 = jnp.where(lane == i1, -1.0, eprob)
    p2 = rmax(eprob2)
    i2 = rmin(jnp.where(eprob2 == p2, lane, big))
    w1 = g_p * p1 / (p1 + p2)
    w2 = g_p * p2 / (p1 + p2)

    sel1 = lane == i1
    sel2 = lane == i2
    onehot = jnp.where(sel1 | sel2, 1.0, 0.0)
    before = _dot(ltri_ref[...], onehot.astype(BF16)) + cnt_scr[...]
    rank1 = rsum(jnp.where(sel1, before, 0.0))
    rank2 = rsum(jnp.where(sel2, before, 0.0))
    cnt_scr[...] = cnt_scr[...] + jnp.sum(onehot, axis=0, keepdims=True)
    counts_ref[...] = cnt_scr[...]

    rec = jnp.where(lane == 0, i1 - first, 0.0)
    rec = jnp.where(lane == 1, i2 - first, rec)
    rec = jnp.where(lane == 2, w1, rec)
    rec = jnp.where(lane == 3, w2, rec)
    rec = jnp.where(lane == 4, rank1, rec)
    rec = jnp.where(lane == 5, rank2, rec)
    route_ref[...] = rec
    route_t_ref[...] = rec.T[:ROUTE_ROWS, :]


def _out_router(mixed2d, x2d, wout_bf16, g, wr_hi, wr_lo, br, ltri):
    t, d = x2d.shape
    tm = ltri.shape[0]
    c2 = lambda i: (0, 0)
    return pl.pallas_call(
        _out_router_kernel,
        grid=(t // tm,),
        in_specs=[
            pl.BlockSpec((tm, d), lambda i: (i, 0)),
            pl.BlockSpec((tm, d), lambda i: (i, 0)),
            pl.BlockSpec((d, d), c2),
            pl.BlockSpec((1, d), c2),
            pl.BlockSpec((d, ROUTE_LANES), c2),
            pl.BlockSpec((d, ROUTE_LANES), c2),
            pl.BlockSpec((1, ROUTE_LANES), c2),
            pl.BlockSpec((tm, tm), c2),
        ],
        out_specs=[
            pl.BlockSpec((tm, d), lambda i: (i, 0)),
            pl.BlockSpec((tm * ROW_TILES, LANES), lambda i: (i, 0)),
            pl.BlockSpec((tm, ROUTE_LANES), lambda i: (i, 0)),
            pl.BlockSpec((ROUTE_ROWS, tm), lambda i: (0, i)),
            pl.BlockSpec((1, ROUTE_LANES), c2),
        ],
        out_shape=[
            jax.ShapeDtypeStruct((t, d), F32),
            jax.ShapeDtypeStruct((t * ROW_TILES, LANES), F32),
            jax.ShapeDtypeStruct((t, ROUTE_LANES), F32),
            jax.ShapeDtypeStruct((ROUTE_ROWS, t), F32),
            jax.ShapeDtypeStruct((1, ROUTE_LANES), F32),
        ],
        scratch_shapes=[pltpu.VMEM((1, ROUTE_LANES), F32)],
        compiler_params=pltpu.CompilerParams(
            dimension_semantics=("arbitrary",), vmem_limit_bytes=VMEM_LIMIT),
        name="out_router",
    )(mixed2d, x2d, wout_bf16, g, wr_hi, wr_lo, br, ltri)


def _row_copy(src_ref, src_row, dst_ref, dst_row, sem):
    return pltpu.make_async_copy(_row_tile(src_ref, src_row), _row_tile(dst_ref, dst_row), sem)


def _dispatch_kernel(pad_end_ref, padded_ref, dest_ref, u2_ref, xrows_ref, zero_scr, zsem, sem):
    i = pl.program_id(0)
    tm = u2_ref.shape[0] // ROW_TILES
    blk = BM_MOE * ROW_TILES
    n_blocks = xrows_ref.shape[0] // blk
    n_used = pad_end_ref[N_EXPERTS - 1] // BM_MOE

    def zero_block(start):
        return pltpu.make_async_copy(
            zero_scr, xrows_ref.at[pl.ds(pl.multiple_of(start * ROW_TILES, blk), blk), :], zsem)

    def zero_copy(e):
        return zero_block(pad_end_ref[e] - BM_MOE)

    @pl.when(i == 0)
    def _():
        zero_scr[...] = jnp.zeros_like(zero_scr)
        for e in range(N_EXPERTS):
            @pl.when(padded_ref[e] > 0)
            def _():
                zero_copy(e).start()

        def tail_start(j, carry):
            zero_block(j * BM_MOE).start()
            return carry

        lax.fori_loop(n_used, n_blocks, tail_start, 0)
        for e in range(N_EXPERTS):
            @pl.when(padded_ref[e] > 0)
            def _():
                zero_copy(e).wait()

        def tail_wait(j, carry):
            zero_block(j * BM_MOE).wait()
            return carry

        lax.fori_loop(n_used, n_blocks, tail_wait, 0)

    def issue(jj, carry):
        for u in range(DMA_UNROLL):
            j = jj * DMA_UNROLL + u
            _row_copy(u2_ref, j, xrows_ref, dest_ref[0, 0, j], sem).start(priority=0)
            _row_copy(u2_ref, j, xrows_ref, dest_ref[0, 1, j], sem).start(priority=1)
        return carry

    lax.fori_loop(0, tm // DMA_UNROLL, issue, 0)

    def drain(j, carry):
        _row_copy(u2_ref, j, xrows_ref, dest_ref[0, 0, j], sem).wait()
        _row_copy(u2_ref, j, xrows_ref, dest_ref[0, 1, j], sem).wait()
        return carry

    lax.fori_loop(0, tm, drain, 0, unroll=DMA_UNROLL)


def _dispatch(pad_end, padded, dest3, u2, n_rows):
    t = u2.shape[0] // ROW_TILES
    tm = dest3.shape[2]
    grid_spec = pltpu.PrefetchScalarGridSpec(
        num_scalar_prefetch=2,
        grid=(t // tm,),
        in_specs=[
            pl.BlockSpec((1, 2, tm), lambda i, pe, pd: (i, 0, 0), memory_space=pltpu.SMEM),
            pl.BlockSpec((tm * ROW_TILES, LANES), lambda i, pe, pd: (i, 0)),
        ],
        out_specs=pl.BlockSpec(memory_space=pl.ANY),
        scratch_shapes=[
            pltpu.VMEM((BM_MOE * ROW_TILES, LANES), F32),
            pltpu.SemaphoreType.DMA(()),
            pltpu.SemaphoreType.DMA(()),
        ],
    )
    return pl.pallas_call(
        _dispatch_kernel,
        grid_spec=grid_spec,
        out_shape=jax.ShapeDtypeStruct((n_rows * ROW_TILES, LANES), F32),
        compiler_params=pltpu.CompilerParams(
            dimension_semantics=("arbitrary",), vmem_limit_bytes=VMEM_LIMIT),
        name="dispatch",
    )(pad_end, padded, dest3, u2)


def _expert_ffn_kernel(be_ref, nused_ref, x_ref, wg_ref, wu_ref, wd_ref, y_ref, wg_s, wu_s, wd_s):
    i = pl.program_id(0)
    used = i < nused_ref[0]

    @pl.when(used & ((i == 0) | (be_ref[i] != be_ref[jnp.maximum(i - 1, 0)])))
    def _():
        wg_s[...] = wg_ref[0].astype(BF16)
        wu_s[...] = wu_ref[0].astype(BF16)
        wd_s[...] = wd_ref[0].astype(BF16)

    @pl.when(used)
    def _():
        x = _load_rows(x_ref).astype(BF16)
        hg = _dot(x, wg_s[...])
        hu = _dot(x, wu_s[...])
        hid = (hg * jax.nn.sigmoid(hg)) * hu
        _store_rows(y_ref, _dot(hid.astype(BF16), wd_s[...]))

    @pl.when(jnp.logical_not(used))
    def _():
        y_ref[...] = jnp.zeros_like(y_ref)


def _expert_ffn(block_expert, n_used, x_rows, wg, wu, wd):
    blk = BM_MOE * ROW_TILES
    n_blocks = x_rows.shape[0] // blk
    d, de = wg.shape[1], wg.shape[2]

    def row_map(i, be, nu):
        return (jnp.minimum(i, nu[0] - 1), 0)

    def w_map(i, be, nu):
        return (be[jnp.minimum(i, nu[0] - 1)], 0, 0)

    grid_spec = pltpu.PrefetchScalarGridSpec(
        num_scalar_prefetch=2,
        grid=(n_blocks,),
        in_specs=[
            pl.BlockSpec((blk, LANES), row_map),
            pl.BlockSpec((1, d, de), w_map),
            pl.BlockSpec((1, d, de), w_map),
            pl.BlockSpec((1, de, d), w_map),
        ],
        out_specs=pl.BlockSpec((blk, LANES), lambda i, be, nu: (i, 0)),
        scratch_shapes=[pltpu.VMEM((d, de), BF16), pltpu.VMEM((d, de), BF16), pltpu.VMEM((de, d), BF16)],
    )
    return pl.pallas_call(
        _expert_ffn_kernel,
        grid_spec=grid_spec,
        out_shape=jax.ShapeDtypeStruct(x_rows.shape, F32),
        compiler_params=pltpu.CompilerParams(
            dimension_semantics=("arbitrary",), vmem_limit_bytes=VMEM_LIMIT),
        name="expert_ffn",
    )(block_expert, n_used, x_rows, wg, wu, wd)


def _combine_kernel(dest_ref, dnext_ref, h1_ref, route_ref, g_ref, yrows_ref, out_ref, ybuf, sems):
    i = pl.program_id(0)
    n = pl.num_programs(0)
    tm = h1_ref.shape[0]
    slot = lax.rem(i, 2)

    def copies(dref, s, j):
        return (_row_copy(yrows_ref, dref[0, 0, j], ybuf.at[s, 0], j, sems.at[s]),
                _row_copy(yrows_ref, dref[0, 1, j], ybuf.at[s, 1], j, sems.at[s]))

    def issue_tile(dref, s):
        def body(jj, carry):
            for u in range(DMA_UNROLL):
                c0, c1 = copies(dref, s, jj * DMA_UNROLL + u)
                c0.start(priority=0)
                c1.start(priority=1)
            return carry

        lax.fori_loop(0, tm // DMA_UNROLL, body, 0)

    @pl.when(i == 0)
    def _():
        issue_tile(dest_ref, 0)

    @pl.when(i + 1 < n)
    def _():
        issue_tile(dnext_ref, 1 - slot)

    def drain(j, carry):
        c0, c1 = copies(dest_ref, slot, j)
        c0.wait()
        c1.wait()
        return carry

    lax.fori_loop(0, tm, drain, 0, unroll=DMA_UNROLL)

    rec = route_ref[...]
    w1 = rec[:, 2:3]
    w2 = rec[:, 3:4]
    h = h1_ref[...] + _load_rows(ybuf.at[slot, 0]) * w1 + _load_rows(ybuf.at[slot, 1]) * w2
    out_ref[...] = _rms(h, g_ref[...])


def _combine(dest3, h1, route, g, y_rows):
    t, d = h1.shape
    tm = dest3.shape[2]
    n = t // tm
    return pl.pallas_call(
        _combine_kernel,
        grid=(n,),
        in_specs=[
            pl.BlockSpec((1, 2, tm), lambda i: (i, 0, 0), memory_space=pltpu.SMEM),
            pl.BlockSpec((1, 2, tm), lambda i: (jnp.minimum(i + 1, n - 1), 0, 0), memory_space=pltpu.SMEM),
            pl.BlockSpec((tm, d), lambda i: (i, 0)),
            pl.BlockSpec((tm, ROUTE_LANES), lambda i: (i, 0)),
            pl.BlockSpec((1, d), lambda i: (0, 0)),
            pl.BlockSpec(memory_space=pl.ANY),
        ],
        out_specs=pl.BlockSpec((tm, d), lambda i: (i, 0)),
        out_shape=jax.ShapeDtypeStruct((t, d), F32),
        scratch_shapes=[pltpu.VMEM((2, 2, tm * ROW_TILES, LANES), F32), pltpu.SemaphoreType.DMA((2,))],
        compiler_params=pltpu.CompilerParams(
            dimension_semantics=("arbitrary",), vmem_limit_bytes=VMEM_LIMIT),
        name="combine",
    )(dest3, dest3, h1, route, g, y_rows)


def _rope_tables(positions):
    inv_freq = ROPE_BASE ** (-jnp.arange(0, HEAD_B, 2, dtype=F32) / HEAD_B)
    ang = positions.astype(F32)[:, None] * inv_freq[None, :]
    cos, sin = jnp.cos(ang), jnp.sin(ang)
    return jnp.concatenate([cos, cos], axis=1), jnp.concatenate([-sin, sin], axis=1)


def _hi_lo(w):
    hi = w.astype(BF16)
    return hi, (w - hi.astype(F32)).astype(BF16)


def kernel(x, meta_tokens, norm_mix, w_in, shift_mu, decay_w0, decay_up, iclr_a0, iclr_up, gate_up, k_k, k_a, r_k, ln_w_a, ln_b_a, gn_w_b, w_out, norm_ffn, router_group_w, router_group_b, router_expert_w, router_expert_b, moe_w_gate, moe_w_up, moe_w_down, norm_final):
    bsz, seq, d = x.shape
    assert d == D_MODEL and seq % CHUNK == 0 and norm_mix.shape[0] == 1
    t = bsz * seq
    li = 0

    w_in_b = w_in[li].astype(BF16)
    g_mix = norm_mix[li][None, :]
    mu = shift_mu[li][None, :]
    pvec = jnp.stack([decay_w0[li], iclr_a0[li], k_k[li], k_a[li], r_k[li], ln_w_a[li], ln_b_a[li], gn_w_b[li]])
    wwa = jnp.zeros((LORA_W + LORA_A, 2 * D_A), F32)
    wwa = wwa.at[:LORA_W, :D_A].set(decay_up[li]).at[LORA_W:, D_A:].set(iclr_up[li])
    wwa_hi, wwa_lo = _hi_lo(wwa)
    wg_hi, wg_lo = _hi_lo(gate_up[li])
    ch = jnp.arange(HEAD_SUM_W) // HEAD_A
    bd = (ch[:, None] == ch[None, :]).astype(BF16)

    meta_pad = jnp.concatenate([jnp.zeros((CHUNK - N_META, d), F32), meta_tokens.astype(F32)], axis=0)
    z_meta = _in_proj(meta_pad, g_mix, w_in_b)
    cos_m, sin_m = _rope_tables(jnp.arange(CHUNK) - (CHUNK - N_META))
    zeros_a = jnp.zeros((N_PAIRS, LANES, LANES), F32)
    zeros_b = jnp.zeros((H_B, HEAD_B, HEAD_B), F32)
    _, sa_meta, rb_meta = _mixer(z_meta[None], jnp.zeros((1, P_A), F32), zeros_a, zeros_b, cos_m, sin_m,
                                 mu, pvec, wwa_hi, wwa_lo, wg_hi, wg_lo, bd)

    x2d = x.reshape(t, d)
    z = _in_proj(x2d, g_mix, w_in_b)
    cos_x, sin_x = _rope_tables(N_META + jnp.arange(seq))
    mixed, _, _ = _mixer(z.reshape(bsz, seq, P_IN), z_meta[CHUNK - 1:CHUNK, :P_A], sa_meta[0], rb_meta[0],
                         cos_x, sin_x, mu, pvec, wwa_hi, wwa_lo, wg_hi, wg_lo, bd)

    tm_r = min(TM_ROUTE, t)
    wr = jnp.zeros((d, ROUTE_LANES), F32)
    wr = wr.at[:, :N_GROUPS].set(router_group_w[li]).at[:, N_GROUPS:N_GROUPS + N_EXPERTS].set(router_expert_w[li])
    wr_hi, wr_lo = _hi_lo(wr)
    br = jnp.zeros((1, ROUTE_LANES), F32)
    br = br.at[0, :N_GROUPS].set(router_group_b[li]).at[0, N_GROUPS:N_GROUPS + N_EXPERTS].set(router_expert_b[li])
    ii = jnp.arange(tm_r)
    ltri = (ii[None, :] < ii[:, None]).astype(BF16)
    h1, u2, route, route_t, counts = _out_router(mixed.reshape(t, d), x2d, w_out[li].astype(BF16), norm_ffn[li][None, :],
                                        wr_hi, wr_lo, br, ltri)

    n_blocks = -(-(2 * t + N_EXPERTS * (BM_MOE - 1)) // BM_MOE)
    n_rows = n_blocks * BM_MOE
    cnt = counts[0, N_GROUPS:N_GROUPS + N_EXPERTS].astype(jnp.int32)
    padded = (cnt + BM_MOE - 1) // BM_MOE * BM_MOE
    pad_end = jnp.cumsum(padded)
    pad_start = pad_end - padded
    n_used = (pad_end[-1:] // BM_MOE).astype(jnp.int32)
    blk_start = jnp.arange(n_blocks, dtype=jnp.int32) * BM_MOE
    block_expert = jnp.minimum(jnp.sum(pad_end[None, :] <= blk_start[:, None], axis=1), N_EXPERTS - 1).astype(jnp.int32)
    eids = route_t[0:2].astype(jnp.int32)
    ranks = route_t[4:6].astype(jnp.int32)
    dest = pad_start[eids] + ranks

    def tiled(tm):
        return dest.reshape(2, t // tm, tm).transpose(1, 0, 2)

    x_rows = _dispatch(pad_end.astype(jnp.int32), padded.astype(jnp.int32), tiled(tm_r), u2, n_rows)
    y_rows = _expert_ffn(block_expert, n_used, x_rows, moe_w_gate[li], moe_w_up[li], moe_w_down[li])
    out = _combine(tiled(min(TM_COMB, t)), h1, route, norm_final[None, :], y_rows)
    return out.reshape(bsz, seq, d)
```

```python
import functools
import math

import jax
import jax.numpy as jnp
from jax import lax
from jax.experimental import pallas as pl
from jax.experimental.pallas import tpu as pltpu

F32 = jnp.float32
BF16 = jnp.bfloat16

D_MODEL = 1024
CHUNK = 64
LOG2_CHUNK = 6
N_META = 16
D_A = 512
HEAD_A = 64
LORA_W = 64
LORA_A = 64
LORA_G = 128
DECAY_SCALE = math.exp(-0.5)
GN_EPS_A = 64e-5
D_B = 512
H_B = 4
HEAD_B = 128
ROPE_BASE = 10000.0
GN_EPS_B = 1e-5
P_A = 3 * D_A + LORA_W + LORA_A + LORA_G
P_B = 4 * D_B
P_IN = P_A + P_B
N_GROUPS = 4
EXPERTS_PER_GROUP = 8
N_EXPERTS = N_GROUPS * EXPERTS_PER_GROUP
D_EXPERT = 512
NORM_EPS = 1e-6

LANES = 128
N_PAIRS = D_A // LANES
ROW_TILES = D_MODEL // LANES
HEAD_SUM_W = 256
VMEM_LIMIT = 48 * 1024 * 1024

ROWS_MIX = 4
GROUP_ROWS = 2
MIXER_PHASES = 3
TM_PROJ = 512
TM_ROUTE = 512
TM_COMB = 256
BM_MOE = 512
ROUTE_LANES = LANES
ROUTE_ROWS = 8
DMA_UNROLL = 8


def _dot(a, b):
    return jnp.dot(a, b, preferred_element_type=F32)


def _dot_nt(a, b):
    return lax.dot_general(a, b, (((1,), (1,)), ((), ())), preferred_element_type=F32)


def _dot_tn(a, b):
    return lax.dot_general(a, b, (((0,), (0,)), ((), ())), preferred_element_type=F32)


def _split2(x):
    hi = x.astype(BF16)
    lo = (x - hi.astype(F32)).astype(BF16)
    return hi, lo


def _split3(x):
    x1 = x.astype(BF16)
    r1 = x - x1.astype(F32)
    x2 = r1.astype(BF16)
    x3 = (r1 - x2.astype(F32)).astype(BF16)
    return x1, x2, x3


def _dot_x2(x, w_exact):
    hi, lo = _split2(x)
    return _dot(hi, w_exact) + _dot(lo, w_exact)


def _dot_3pass(x, w_hi, w_lo):
    hi, lo = _split2(x)
    return _dot(hi, w_hi) + _dot(lo, w_hi) + _dot(hi, w_lo)


def _rms(x, g):
    return x * lax.rsqrt(jnp.mean(x * x, axis=-1, keepdims=True) + NORM_EPS) * g


def _store_rows(ref, x):
    n = ref.shape[0] // ROW_TILES
    for c in range(ROW_TILES):
        ref[pl.ds(c, n, stride=ROW_TILES), :] = x[:, c * LANES:(c + 1) * LANES]


def _load_rows(ref):
    n = ref.shape[0] // ROW_TILES
    return jnp.concatenate([ref[pl.ds(c, n, stride=ROW_TILES), :] for c in range(ROW_TILES)], axis=1)


def _row_tile(ref, row):
    return ref.at[pl.ds(pl.multiple_of(row * ROW_TILES, ROW_TILES), ROW_TILES), :]


def _in_proj_kernel(x_ref, g_ref, w_ref, z_ref):
    u = _rms(x_ref[...], g_ref[...])
    z_ref[...] = _dot(u.astype(BF16), w_ref[...])


def _in_proj(x2d, g, w_bf16):
    m, d = x2d.shape
    tm = min(TM_PROJ, m)
    n = w_bf16.shape[1]
    return pl.pallas_call(
        _in_proj_kernel,
        grid=(m // tm,),
        in_specs=[
            pl.BlockSpec((tm, d), lambda i: (i, 0)),
            pl.BlockSpec((1, d), lambda i: (0, 0)),
            pl.BlockSpec((d, n), lambda i: (0, 0)),
        ],
        out_specs=pl.BlockSpec((tm, n), lambda i: (i, 0)),
        out_shape=jax.ShapeDtypeStruct((m, n), F32),
        compiler_params=pltpu.CompilerParams(
            dimension_semantics=("arbitrary",), vmem_limit_bytes=VMEM_LIMIT),
        name="in_proj",
    )(x2d, g, w_bf16)


def _mixer_kernel(z_ref, zprev0_ref, sa0_ref, rb0_ref, cos_ref, sin_ref, mu_ref, pvec_ref,
                  wwa_hi_ref, wwa_lo_ref, wg_hi_ref, wg_lo_ref, bd_ref,
                  mixed_ref, sa_out_ref, rb_out_ref,
                  sa_scr, rb_scr, zlast_scr, *, rows):
    c_idx = pl.program_id(1)
    n_chunks = pl.num_programs(1)

    @pl.when(c_idx == 0)
    def _():
        for rr in range(rows):
            sa_scr[rr] = sa0_ref[...]
            rb_scr[rr] = rb0_ref[...]
            zlast_scr[rr] = zprev0_ref[...]

    groups = [list(range(g0, min(g0 + GROUP_ROWS, rows))) for g0 in range(0, rows, GROUP_ROWS)]
    phases = [_mixer_group(z_ref, cos_ref, sin_ref, mu_ref, pvec_ref, wwa_hi_ref, wwa_lo_ref, wg_hi_ref,
                           wg_lo_ref, bd_ref, mixed_ref, sa_scr, rb_scr, zlast_scr, grp) for grp in groups]
    for _ in range(MIXER_PHASES):
        for ph in phases:
            next(ph)

    @pl.when(c_idx == n_chunks - 1)
    def _():
        sa_out_ref[...] = sa_scr[...]
        rb_out_ref[...] = rb_scr[...]


def _mixer_group(z_ref, cos_ref, sin_ref, mu_ref, pvec_ref, wwa_hi_ref, wwa_lo_ref, wg_hi_ref, wg_lo_ref,
                 bd_ref, mixed_ref, sa_scr, rb_scr, zlast_scr, grp):
    C = CHUNK
    R = len(grp)
    M = R * C
    rws = [slice(rr * C, (rr + 1) * C) for rr in range(R)]

    z = jnp.concatenate([z_ref[grp[rr]] for rr in range(R)], axis=0)
    za = z[:, :P_A]
    row = lax.broadcasted_iota(jnp.int32, (M, 1), 0)
    zprev = pltpu.roll(za, 1, 0)
    for rr in range(R):
        zprev = jnp.where(row == rr * C, zlast_scr[grp[rr]], zprev)
        zlast_scr[grp[rr]] = za[(rr + 1) * C - 1:(rr + 1) * C, :]
    zs = za + (zprev - za) * mu_ref[...]

    r = zs[:, 0:D_A]
    k = zs[:, D_A:2 * D_A]
    v = zs[:, 2 * D_A:3 * D_A]
    wa = zs[:, 3 * D_A:3 * D_A + LORA_W + LORA_A]
    g_lo = zs[:, 3 * D_A + LORA_W + LORA_A:P_A]

    decay_w0 = pvec_ref[0:1, :]
    iclr_a0 = pvec_ref[1:2, :]
    k_k = pvec_ref[2:3, :]
    k_a = pvec_ref[3:4, :]
    r_k = pvec_ref[4:5, :]
    ln_w = pvec_ref[5:6, :]
    ln_b = pvec_ref[6:7, :]
    gn_w = pvec_ref[7:8, :]

    lane = lax.broadcasted_iota(jnp.int32, (1, LANES), 1)
    wa_act = jnp.where(lane < LORA_W, jnp.tanh(wa), wa)
    pre = _dot_3pass(wa_act, wwa_hi_ref[...], wwa_lo_ref[...])
    log_w = -DECAY_SCALE * jax.nn.sigmoid(decay_w0 + pre[:, :D_A])
    a = jax.nn.sigmoid(iclr_a0 + pre[:, D_A:])
    g = _dot_3pass(jax.nn.sigmoid(g_lo), wg_hi_ref[...], wg_lo_ref[...])

    bd = bd_ref[...]

    def head_sum(t):
        return jnp.concatenate([_dot(t[:, j * HEAD_SUM_W:(j + 1) * HEAD_SUM_W], bd)
                                for j in range(D_A // HEAD_SUM_W)], axis=1)

    kk = k * k_k
    kk = kk / jnp.maximum(jnp.sqrt(head_sum((kk * kk).astype(BF16))), 1e-12)
    k2 = k * (1.0 + (a - 1.0) * k_a)
    b = kk * a

    tm_i = lax.broadcasted_iota(jnp.int32, (M, M), 0)
    sm_j = lax.broadcasted_iota(jnp.int32, (M, M), 1)
    same_row = lax.shift_right_logical(tm_i, LOG2_CHUNK) == lax.shift_right_logical(sm_j, LOG2_CHUNK)
    tril = ((sm_j <= tm_i) & same_row).astype(BF16)
    lw_hi, lw_lo = _split2(log_w)
    cum = _dot(tril, lw_hi) + _dot(tril, lw_lo)
    cmid_r = [cum[rr * C + C // 2 - 1:rr * C + C // 2, :] for rr in range(R)]
    cc = cum - jnp.concatenate([jnp.broadcast_to(cm, (C, D_A)) for cm in cmid_r], axis=0)
    e_nc = jnp.exp(-cc)
    rg = r * jnp.exp(cc)
    kkg = kk * jnp.exp(cc - log_w)
    kinv = k2 * e_nc
    binv = b * e_nc
    e_mid = [jnp.exp(cm) for cm in cmid_r]
    e_end = [jnp.exp(cc[(rr + 1) * C - 1:(rr + 1) * C, :]) for rr in range(R)]
    gam = [jnp.exp(cum[(rr + 1) * C - 1:(rr + 1) * C, :]) for rr in range(R)]

    i2 = lax.broadcasted_iota(jnp.int32, (2 * C, 2 * C), 0)
    j2 = lax.broadcasted_iota(jnp.int32, (2 * C, 2 * C), 1)
    bi = i2 >= C
    bj = j2 >= C
    t2 = jnp.where(bi, i2 - C, i2)
    s2 = jnp.where(bj, j2 - C, j2)
    same_blk = bi == bj
    strict_same = (s2 < t2) & same_blk
    strict_cross = (s2 < t2) & jnp.logical_not(same_blk)
    ta = lax.broadcasted_iota(jnp.int32, (C, 4 * C), 0)
    ja = lax.broadcasted_iota(jnp.int32, (C, 4 * C), 1)
    incl = (ja & (C - 1)) <= ta
    m0 = lane < HEAD_A
    m1 = jnp.logical_not(m0)

    yield

    items =[(rr, p) for rr in range(R) for p in range(N_PAIRS)]
    n_it = range(len(items))
    rsl = [rws[rr] for rr, _ in items]
    lsl = [slice(p * LANES, (p + 1) * LANES) for _, p in items]
    kkg0 = [jnp.where(m0, kkg[rsl[i], lsl[i]], 0.0) for i in n_it]
    kkg1 = [jnp.where(m1, kkg[rsl[i], lsl[i]], 0.0) for i in n_it]
    kinv_i = [kinv[rsl[i], lsl[i]] for i in n_it]
    binv_i = [binv[rsl[i], lsl[i]] for i in n_it]
    v_i = [v[rsl[i], lsl[i]] for i in n_it]
    lhs = [jnp.concatenate([kkg[rsl[i], lsl[i]], rg[rsl[i], lsl[i]]], axis=0).astype(BF16) for i in n_it]
    rhs = [jnp.concatenate([jnp.where(m0, binv_i[i], 0.0), jnp.where(m0, kinv_i[i], 0.0),
                            jnp.where(m1, kinv_i[i], 0.0), jnp.where(m1, binv_i[i], 0.0)], axis=0).astype(BF16)
           for i in n_it]
    out = [_dot_nt(lhs[i], rhs[i]) for i in n_it]
    top = [jnp.concatenate([out[i][:C, :2 * C], out[i][:C, 2 * C:]], axis=0) for i in n_it]
    npow = [jnp.where(strict_same, -top[i], 0.0) for i in n_it]
    q_anti = [jnp.where(strict_cross, top[i], 0.0) for i in n_it]
    vm0 = [jnp.where(m0, v_i[i], 0.0) for i in n_it]
    vm1 = [jnp.where(m1, v_i[i], 0.0) for i in n_it]
    qv = [_dot(q_anti[i].astype(BF16), jnp.concatenate([vm1[i], vm0[i]], axis=0).astype(BF16)) for i in n_it]
    x = [qv[i] + pltpu.roll(jnp.concatenate([kkg0[i], kkg1[i]], axis=0), HEAD_A, 1) for i in n_it]
    for it in range(6):
        nb = [npow[i].astype(BF16) for i in n_it]
        if it < 5:
            prod = [_dot(nb[i], jnp.concatenate([x[i].astype(BF16), nb[i]], axis=1)) for i in n_it]
            x = [x[i] + prod[i][:, :LANES] for i in n_it]
            npow = [prod[i][:, LANES:] for i in n_it]
        else:
            x = [x[i] + _dot(nb[i], x[i].astype(BF16)) for i in n_it]
    w_stack = [jnp.where(same_blk, x[i], 0.0) for i in n_it]
    kkt_stack = [pltpu.roll(jnp.where(same_blk, 0.0, x[i]), HEAD_A, 1) for i in n_it]
    s_old = [sa_scr[grp[rr], p] for rr, p in items]
    s0m = [(s_old[i] * e_mid[items[i][0]][:, lsl[i]]).astype(BF16) for i in n_it]
    u_stack = [_dot_nt(kkt_stack[i].astype(BF16), s0m[i]) + w_stack[i] for i in n_it]
    y_it = [_dot_nt(rg[rsl[i], lsl[i]].astype(BF16), s0m[i]) for i in n_it]
    a_cat = [jnp.where(incl, out[i][C:], 0.0).astype(BF16) for i in n_it]
    y_it = [y_it[i] + _dot(a_cat[i], jnp.concatenate([-u_stack[i][:C], vm0[i], vm1[i], -u_stack[i][C:]],
                                                     axis=0).astype(BF16)) for i in n_it]
    gt = [_dot_tn(jnp.concatenate([v_i[i], -(u_stack[i][:C] + u_stack[i][C:])], axis=0).astype(BF16),
                  jnp.concatenate([kinv_i[i], binv_i[i]], axis=0).astype(BF16)) for i in n_it]
    for i in n_it:
        rr, p = items[i]
        sa_scr[grp[rr], p] = s_old[i] * gam[rr][:, lsl[i]] + jnp.where(same_blk, gt[i], 0.0) * e_end[rr][:, lsl[i]]

    yield

    y = jnp.concatenate([jnp.concatenate(y_it[rr * N_PAIRS:(rr + 1) * N_PAIRS], axis=1) for rr in range(R)],
                        axis=0)
    inv_n = 1.0 / HEAD_A
    mean = head_sum(y.astype(BF16)) * inv_n
    dlt = y - mean
    var = head_sum((dlt * dlt).astype(BF16)) * inv_n
    yn = dlt * lax.rsqrt(var + GN_EPS_A)
    bonus = head_sum((r * k2 * r_k).astype(BF16)) * v
    out_a = (yn * ln_w + ln_b + bonus) * g

    zb = z[:, P_A:]
    cosf = cos_ref[...]
    sinf = sin_ref[...]
    ti = lax.broadcasted_iota(jnp.int32, (C, C), 0)
    sj = lax.broadcasted_iota(jnp.int32, (C, C), 1)
    relf = (ti - sj).astype(F32)
    causal = sj <= ti
    rowf = lax.broadcasted_iota(jnp.int32, (C, 1), 0).astype(F32)
    hb = range(H_B)
    lgs = [math.log1p(-(2.0 ** (-5.0 - h))) for h in hb]
    decay_in = [jnp.where(causal, jnp.exp(lg * jnp.maximum(relf, 0.0)), 0.0) for lg in lgs]
    q_dec = [jnp.exp(lg * (rowf + 1.0)) for lg in lgs]
    k_dec = [jnp.exp(lg * (float(C - 1) - rowf)) for lg in lgs]
    bitems = [(rr, h) for rr in range(R) for h in hb]
    n_b = range(len(bitems))

    def zcol(i, part):
        rr, h = bitems[i]
        return zb[rws[rr], part * D_B + h * HEAD_B:part * D_B + (h + 1) * HEAD_B]

    def rope(t):
        return t * cosf + pltpu.roll(t, HEAD_B // 2, 1) * sinf

    q_b = [rope(zcol(i, 0)).astype(BF16) for i in n_b]
    k_r = [rope(zcol(i, 1)) * (HEAD_B ** -0.5) for i in n_b]
    v_b = [zcol(i, 2).astype(BF16) for i in n_b]
    r_old = [rb_scr[grp[rr], h] for rr, h in bitems]
    qrk = [_dot_nt(q_b[i], jnp.concatenate([r_old[i].astype(BF16), k_r[i].astype(BF16)], axis=0)) for i in n_b]
    cross = [qrk[i][:, :HEAD_B] * q_dec[bitems[i][1]] for i in n_b]
    scores = [qrk[i][:, HEAD_B:] * decay_in[bitems[i][1]] for i in n_b]
    inner = [_dot(scores[i].astype(BF16), v_b[i]) for i in n_b]
    kv = [_dot_tn(v_b[i], (k_r[i] * k_dec[bitems[i][1]]).astype(BF16)) for i in n_b]
    outs_b = []
    for i in n_b:
        rr, h = bitems[i]
        rb_scr[grp[rr], h] = r_old[i] * math.exp(lgs[h] * C) + kv[i]
        y_h = inner[i] + cross[i]
        mu_h = jnp.mean(y_h, axis=-1, keepdims=True)
        d_h = y_h - mu_h
        var_h = jnp.mean(d_h * d_h, axis=-1, keepdims=True)
        yn_h = d_h * lax.rsqrt(var_h + GN_EPS_B)
        g_h = zcol(i, 3)
        outs_b.append(yn_h * gn_w[:, h * HEAD_B:(h + 1) * HEAD_B] * (g_h * jax.nn.sigmoid(g_h)))
    out_b = jnp.concatenate([jnp.concatenate(outs_b[rr * H_B:(rr + 1) * H_B], axis=1) for rr in range(R)], axis=0)

    mixed = jnp.concatenate([out_a, out_b], axis=1).astype(mixed_ref.dtype)
    for rr in range(R):
        mixed_ref[grp[rr]] = mixed[rws[rr]]

    yield


def _mixer(z3, zprev0, sa0, rb0, cosf, sinf, mu, pvec, wwa_hi, wwa_lo, wg_hi, wg_lo, bd):
    bsz, length, _ = z3.shape
    n_chunks = length // CHUNK
    rows = ROWS_MIX if bsz % ROWS_MIX == 0 else 1
    const2 = lambda b, c: (0, 0)
    const3 = lambda b, c: (0, 0, 0)
    st_shape = (N_PAIRS, LANES, LANES)
    return pl.pallas_call(
        functools.partial(_mixer_kernel, rows=rows),
        grid=(bsz // rows, n_chunks),
        in_specs=[
            pl.BlockSpec((rows, CHUNK, P_IN), lambda b, c: (b, c, 0)),
            pl.BlockSpec((1, P_A), const2),
            pl.BlockSpec(st_shape, const3),
            pl.BlockSpec((H_B, HEAD_B, HEAD_B), const3),
            pl.BlockSpec((CHUNK, HEAD_B), lambda b, c: (c, 0)),
            pl.BlockSpec((CHUNK, HEAD_B), lambda b, c: (c, 0)),
            pl.BlockSpec((1, P_A), const2),
            pl.BlockSpec((8, D_A), const2),
            pl.BlockSpec((LORA_W + LORA_A, 2 * D_A), const2),
            pl.BlockSpec((LORA_W + LORA_A, 2 * D_A), const2),
            pl.BlockSpec((LORA_G, D_A), const2),
            pl.BlockSpec((LORA_G, D_A), const2),
            pl.BlockSpec((HEAD_SUM_W, HEAD_SUM_W), const2),
        ],
        out_specs=[
            pl.BlockSpec((rows, CHUNK, D_A + D_B), lambda b, c: (b, c, 0)),
            pl.BlockSpec((rows,) + st_shape, lambda b, c: (b, 0, 0, 0)),
            pl.BlockSpec((rows, H_B, HEAD_B, HEAD_B), lambda b, c: (b, 0, 0, 0)),
        ],
        out_shape=[
            jax.ShapeDtypeStruct((bsz, length, D_A + D_B), BF16),
            jax.ShapeDtypeStruct((bsz,) + st_shape, F32),
            jax.ShapeDtypeStruct((bsz, H_B, HEAD_B, HEAD_B), F32),
        ],
        scratch_shapes=[
            pltpu.VMEM((rows,) + st_shape, F32),
            pltpu.VMEM((rows, H_B, HEAD_B, HEAD_B), F32),
            pltpu.VMEM((rows, 1, P_A), F32),
        ],
        compiler_params=pltpu.CompilerParams(
            dimension_semantics=("arbitrary", "arbitrary"), vmem_limit_bytes=VMEM_LIMIT),
        name="mixer",
    )(z3, zprev0, sa0, rb0, cosf, sinf, mu, pvec, wwa_hi, wwa_lo, wg_hi, wg_lo, bd)


def _out_router_kernel(mixed_ref, x_ref, wout_ref, g_ref, wr_hi_ref, wr_lo_ref, br_ref, ltri_ref,
                       h1_ref, u2_ref, route_ref, route_t_ref, counts_ref, cnt_scr):
    i = pl.program_id(0)

    @pl.when(i == 0)
    def _():
        cnt_scr[...] = jnp.zeros_like(cnt_scr)

    h1 = x_ref[...] + _dot(mixed_ref[...], wout_ref[...])
    h1_ref[...] = h1
    u2 = _rms(h1, g_ref[...])
    _store_rows(u2_ref, u2)
    logits =_dot_3pass(u2, wr_hi_ref[...], wr_lo_ref[...]) + br_ref[...]

    tm = logits.shape[0]
    lane = lax.broadcasted_iota(jnp.int32, (tm, ROUTE_LANES), 1).astype(F32)
    neg = -jnp.inf
    big = float(ROUTE_LANES)
    first = float(N_GROUPS)

    def rmax(t):
        return jnp.max(t, axis=-1, keepdims=True)

    def rsum(t):
        return jnp.sum(t, axis=-1, keepdims=True)

    def rmin(t):
        return jnp.min(t, axis=-1, keepdims=True)

    gmask = lane < N_GROUPS
    gmax = rmax(jnp.where(gmask, logits, neg))
    gexp = jnp.where(gmask, jnp.exp(logits - gmax), 0.0)
    gprob = gexp / rsum(gexp)
    g_p = rmax(gprob)
    g_idx = rmin(jnp.where(gmask & (gprob == g_p), lane, big))

    lo_lane = first + EXPERTS_PER_GROUP * g_idx
    emask = (lane >= lo_lane) & (lane < lo_lane + EXPERTS_PER_GROUP)
    emax = rmax(jnp.where(emask, logits, neg))
    eexp = jnp.where(emask, jnp.exp(logits - emax), 0.0)
    eprob = jnp.where(emask, eexp / rsum(eexp), -1.0)
    p1 = rmax(eprob)
    i1 = rmin(jnp.where(eprob == p1, lane, big))
    eprob2 = jnp.where(lane == i1, -1.0, eprob)
    p2 = rmax(eprob2)
    i2 = rmin(jnp.where(eprob2 == p2, lane, big))
    w1 = g_p * p1 / (p1 + p2)
    w2 = g_p * p2 / (p1 + p2)

    sel1 = lane == i1
    sel2 = lane == i2
    onehot = jnp.where(sel1 | sel2, 1.0, 0.0)
    before = _dot(ltri_ref[...], onehot.astype(BF16)) + cnt_scr[...]
    rank1 = rsum(jnp.where(sel1, before, 0.0))
    rank2 = rsum(jnp.where(sel2, before, 0.0))
    cnt_scr[...] = cnt_scr[...] + jnp.sum(onehot, axis=0, keepdims=True)
    counts_ref[...] = cnt_scr[...]

    rec = jnp.where(lane == 0, i1 - first, 0.0)
    rec = jnp.where(lane == 1, i2 - first, rec)
    rec = jnp.where(lane == 2, w1, rec)
    rec = jnp.where(lane == 3, w2, rec)
    rec = jnp.where(lane == 4, rank1, rec)
    rec = jnp.where(lane == 5, rank2, rec)
    route_ref[...] = rec
    route_t_ref[...] = rec.T[:ROUTE_ROWS, :]


def _out_router(mixed2d, x2d, wout_bf16, g, wr_hi, wr_lo, br, ltri):
    t, d = x2d.shape
    tm = ltri.shape[0]
    c2 = lambda i: (0, 0)
    return pl.pallas_call(
        _out_router_kernel,
        grid=(t // tm,),
        in_specs=[
            pl.BlockSpec((tm, d), lambda i: (i, 0)),
            pl.BlockSpec((tm, d), lambda i: (i, 0)),
            pl.BlockSpec((d, d), c2),
            pl.BlockSpec((1, d), c2),
            pl.BlockSpec((d, ROUTE_LANES), c2),
            pl.BlockSpec((d, ROUTE_LANES), c2),
            pl.BlockSpec((1, ROUTE_LANES), c2),
            pl.BlockSpec((tm, tm), c2),
        ],
        out_specs=[
            pl.BlockSpec((tm, d), lambda i: (i, 0)),
            pl.BlockSpec((tm * ROW_TILES, LANES), lambda i: (i, 0)),
            pl.BlockSpec((tm, ROUTE_LANES), lambda i: (i, 0)),
            pl.BlockSpec((ROUTE_ROWS, tm), lambda i: (0, i)),
            pl.BlockSpec((1, ROUTE_LANES), c2),
        ],
        out_shape=[
            jax.ShapeDtypeStruct((t, d), F32),
            jax.ShapeDtypeStruct((t * ROW_TILES, LANES), F32),
            jax.ShapeDtypeStruct((t, ROUTE_LANES), F32),
            jax.ShapeDtypeStruct((ROUTE_ROWS, t), F32),
            jax.ShapeDtypeStruct((1, ROUTE_LANES), F32),
        ],
        scratch_shapes=[pltpu.VMEM((1, ROUTE_LANES), F32)],
        compiler_params=pltpu.CompilerParams(
            dimension_semantics=("arbitrary",), vmem_limit_bytes=VMEM_LIMIT),
        name="out_router",
    )(mixed2d, x2d, wout_bf16, g, wr_hi, wr_lo, br, ltri)


def _row_copy(src_ref, src_row, dst_ref, dst_row, sem):
    return pltpu.make_async_copy(_row_tile(src_ref, src_row), _row_tile(dst_ref, dst_row), sem)


def _dispatch_kernel(pad_end_ref, padded_ref, dest_ref, u2_ref, xrows_ref, zero_scr, zsem, sem):
    i = pl.program_id(0)
    tm = u2_ref.shape[0] // ROW_TILES
    blk = BM_MOE * ROW_TILES
    n_blocks = xrows_ref.shape[0] // blk
    n_used = pad_end_ref[N_EXPERTS - 1] // BM_MOE

    def zero_block(start):
        return pltpu.make_async_copy(
            zero_scr, xrows_ref.at[pl.ds(pl.multiple_of(start * ROW_TILES, blk), blk), :], zsem)

    def zero_copy(e):
        return zero_block(pad_end_ref[e] - BM_MOE)

    @pl.when(i == 0)
    def _():
        zero_scr[...] = jnp.zeros_like(zero_scr)
        for e in range(N_EXPERTS):
            @pl.when(padded_ref[e] > 0)
            def _():
                zero_copy(e).start()

        def tail_start(j, carry):
            zero_block(j * BM_MOE).start()
            return carry

        lax.fori_loop(n_used, n_blocks, tail_start, 0)
        for e in range(N_EXPERTS):
            @pl.when(padded_ref[e] > 0)
            def _():
                zero_copy(e).wait()

        def tail_wait(j, carry):
            zero_block(j * BM_MOE).wait()
            return carry

        lax.fori_loop(n_used, n_blocks, tail_wait, 0)

    def issue(jj, carry):
        for u in range(DMA_UNROLL):
            j = jj * DMA_UNROLL + u
            _row_copy(u2_ref, j, xrows_ref, dest_ref[0, 0, j], sem).start(priority=0)
            _row_copy(u2_ref, j, xrows_ref, dest_ref[0, 1, j], sem).start(priority=1)
        return carry

    lax.fori_loop(0, tm // DMA_UNROLL, issue, 0)

    def drain(j, carry):
        _row_copy(u2_ref, j, xrows_ref, dest_ref[0, 0, j], sem).wait()
        _row_copy(u2_ref, j, xrows_ref, dest_ref[0, 1, j], sem).wait()
        return carry

    lax.fori_loop(0, tm, drain, 0, unroll=DMA_UNROLL)


def _dispatch(pad_end, padded, dest3, u2, n_rows):
    t = u2.shape[0] // ROW_TILES
    tm = dest3.shape[2]
    grid_spec = pltpu.PrefetchScalarGridSpec(
        num_scalar_prefetch=2,
        grid=(t // tm,),
        in_specs=[
            pl.BlockSpec((1, 2, tm), lambda i, pe, pd: (i, 0, 0), memory_space=pltpu.SMEM),
            pl.BlockSpec((tm * ROW_TILES, LANES), lambda i, pe, pd: (i, 0)),
        ],
        out_specs=pl.BlockSpec(memory_space=pl.ANY),
        scratch_shapes=[
            pltpu.VMEM((BM_MOE * ROW_TILES, LANES), F32),
            pltpu.SemaphoreType.DMA(()),
            pltpu.SemaphoreType.DMA(()),
        ],
    )
    return pl.pallas_call(
        _dispatch_kernel,
        grid_spec=grid_spec,
        out_shape=jax.ShapeDtypeStruct((n_rows * ROW_TILES, LANES), F32),
        compiler_params=pltpu.CompilerParams(
            dimension_semantics=("arbitrary",), vmem_limit_bytes=VMEM_LIMIT),
        name="dispatch",
    )(pad_end, padded, dest3, u2)


def _expert_ffn_kernel(be_ref, nused_ref, x_ref, wg_ref, wu_ref, wd_ref, y_ref, wg_s, wu_s, wd_s):
    i = pl.program_id(0)
    used = i < nused_ref[0]

    @pl.when(used & ((i == 0) | (be_ref[i] != be_ref[jnp.maximum(i - 1, 0)])))
    def _():
        wg_s[...] = wg_ref[0].astype(BF16)
        wu_s[...] = wu_ref[0].astype(BF16)
        wd_s[...] = wd_ref[0].astype(BF16)

    @pl.when(used)
    def _():
        x = _load_rows(x_ref).astype(BF16)
        hg = _dot(x, wg_s[...])
        hu = _dot(x, wu_s[...])
        hid = (hg * jax.nn.sigmoid(hg)) * hu
        _store_rows(y_ref, _dot(hid.astype(BF16), wd_s[...]))

    @pl.when(jnp.logical_not(used))
    def _():
        y_ref[...] = jnp.zeros_like(y_ref)


def _expert_ffn(block_expert, n_used, x_rows, wg, wu, wd):
    blk = BM_MOE * ROW_TILES
    n_blocks = x_rows.shape[0] // blk
    d, de = wg.shape[1], wg.shape[2]

    def row_map(i, be, nu):
        return (jnp.minimum(i, nu[0] - 1), 0)

    def w_map(i, be, nu):
        return (be[jnp.minimum(i, nu[0] - 1)], 0, 0)

    grid_spec = pltpu.PrefetchScalarGridSpec(
        num_scalar_prefetch=2,
        grid=(n_blocks,),
        in_specs=[
            pl.BlockSpec((blk, LANES), row_map),
            pl.BlockSpec((1, d, de), w_map),
            pl.BlockSpec((1, d, de), w_map),
            pl.BlockSpec((1, de, d), w_map),
        ],
        out_specs=pl.BlockSpec((blk, LANES), lambda i, be, nu: (i, 0)),
        scratch_shapes=[pltpu.VMEM((d, de), BF16), pltpu.VMEM((d, de), BF16), pltpu.VMEM((de, d), BF16)],
    )
    return pl.pallas_call(
        _expert_ffn_kernel,
        grid_spec=grid_spec,
        out_shape=jax.ShapeDtypeStruct(x_rows.shape, F32),
        compiler_params=pltpu.CompilerParams(
            dimension_semantics=("arbitrary",), vmem_limit_bytes=VMEM_LIMIT),
        name="expert_ffn",
    )(block_expert, n_used, x_rows, wg, wu, wd)


def _combine_kernel(dest_ref, dnext_ref, h1_ref, route_ref, g_ref, yrows_ref, out_ref, ybuf, sems):
    i = pl.program_id(0)
    n = pl.num_programs(0)
    tm = h1_ref.shape[0]
    slot = lax.rem(i, 2)

    def copies(dref, s, j):
        return (_row_copy(yrows_ref, dref[0, 0, j], ybuf.at[s, 0], j, sems.at[s]),
                _row_copy(yrows_ref, dref[0, 1, j], ybuf.at[s, 1], j, sems.at[s]))

    def issue_tile(dref, s):
        def body(jj, carry):
            for u in range(DMA_UNROLL):
                c0, c1 = copies(dref, s, jj * DMA_UNROLL + u)
                c0.start(priority=0)
                c1.start(priority=1)
            return carry

        lax.fori_loop(0, tm // DMA_UNROLL, body, 0)

    @pl.when(i == 0)
    def _():
        issue_tile(dest_ref, 0)

    @pl.when(i + 1 < n)
    def _():
        issue_tile(dnext_ref, 1 - slot)

    def drain(j, carry):
        c0, c1 = copies(dest_ref, slot, j)
        c0.wait()
        c1.wait()
        return carry

    lax.fori_loop(0, tm, drain, 0, unroll=DMA_UNROLL)

    rec = route_ref[...]
    w1 = rec[:, 2:3]
    w2 = rec[:, 3:4]
    h = h1_ref[...] + _load_rows(ybuf.at[slot, 0]) * w1 + _load_rows(ybuf.at[slot, 1]) * w2
    out_ref[...] = _rms(h, g_ref[...])


def _combine(dest3, h1, route, g, y_rows):
    t, d = h1.shape
    tm = dest3.shape[2]
    n = t // tm
    return pl.pallas_call(
        _combine_kernel,
        grid=(n,),
        in_specs=[
            pl.BlockSpec((1, 2, tm), lambda i: (i, 0, 0), memory_space=pltpu.SMEM),
            pl.BlockSpec((1, 2, tm), lambda i: (jnp.minimum(i + 1, n - 1), 0, 0), memory_space=pltpu.SMEM),
            pl.BlockSpec((tm, d), lambda i: (i, 0)),
            pl.BlockSpec((tm, ROUTE_LANES), lambda i: (i, 0)),
            pl.BlockSpec((1, d), lambda i: (0, 0)),
            pl.BlockSpec(memory_space=pl.ANY),
        ],
        out_specs=pl.BlockSpec((tm, d), lambda i: (i, 0)),
        out_shape=jax.ShapeDtypeStruct((t, d), F32),
        scratch_shapes=[pltpu.VMEM((2, 2, tm * ROW_TILES, LANES), F32), pltpu.SemaphoreType.DMA((2,))],
        compiler_params=pltpu.CompilerParams(
            dimension_semantics=("arbitrary",), vmem_limit_bytes=VMEM_LIMIT),
        name="combine",
    )(dest3, dest3, h1, route, g, y_rows)


def _rope_tables(positions):
    inv_freq = ROPE_BASE ** (-jnp.arange(0, HEAD_B, 2, dtype=F32) / HEAD_B)
    ang = positions.astype(F32)[:, None] * inv_freq[None, :]
    cos, sin = jnp.cos(ang), jnp.sin(ang)
    return jnp.concatenate([cos, cos], axis=1), jnp.concatenate([-sin, sin], axis=1)


def _hi_lo(w):
    hi = w.astype(BF16)
    return hi, (w - hi.astype(F32)).astype(BF16)


def kernel(x, meta_tokens, norm_mix, w_in, shift_mu, decay_w0, decay_up, iclr_a0, iclr_up, gate_up, k_k, k_a, r_k, ln_w_a, ln_b_a, gn_w_b, w_out, norm_ffn, router_group_w, router_group_b, router_expert_w, router_expert_b, moe_w_gate, moe_w_up, moe_w_down, norm_final):
    bsz, seq, d = x.shape
    assert d == D_MODEL and seq % CHUNK == 0 and norm_mix.shape[0] == 1
    t = bsz * seq
    li = 0

    w_in_b = w_in[li].astype(BF16)
    g_mix = norm_mix[li][None, :]
    mu = shift_mu[li][None, :]
    pvec = jnp.stack([decay_w0[li], iclr_a0[li], k_k[li], k_a[li], r_k[li], ln_w_a[li], ln_b_a[li], gn_w_b[li]])
    wwa = jnp.zeros((LORA_W + LORA_A, 2 * D_A), F32)
    wwa = wwa.at[:LORA_W, :D_A].set(decay_up[li]).at[LORA_W:, D_A:].set(iclr_up[li])
    wwa_hi, wwa_lo = _hi_lo(wwa)
    wg_hi, wg_lo = _hi_lo(gate_up[li])
    ch = jnp.arange(HEAD_SUM_W) // HEAD_A
    bd = (ch[:, None] == ch[None, :]).astype(BF16)

    meta_pad = jnp.concatenate([jnp.zeros((CHUNK - N_META, d), F32), meta_tokens.astype(F32)], axis=0)
    z_meta = _in_proj(meta_pad, g_mix, w_in_b)
    cos_m, sin_m = _rope_tables(jnp.arange(CHUNK) - (CHUNK - N_META))
    zeros_a = jnp.zeros((N_PAIRS, LANES, LANES), F32)
    zeros_b = jnp.zeros((H_B, HEAD_B, HEAD_B), F32)
    _, sa_meta, rb_meta = _mixer(z_meta[None], jnp.zeros((1, P_A), F32), zeros_a, zeros_b, cos_m, sin_m,
                                 mu, pvec, wwa_hi, wwa_lo, wg_hi, wg_lo, bd)

    x2d = x.reshape(t, d)
    z = _in_proj(x2d, g_mix, w_in_b)
    cos_x, sin_x = _rope_tables(N_META + jnp.arange(seq))
    mixed, _, _ = _mixer(z.reshape(bsz, seq, P_IN), z_meta[CHUNK - 1:CHUNK, :P_A], sa_meta[0], rb_meta[0],
                         cos_x, sin_x, mu, pvec, wwa_hi, wwa_lo, wg_hi, wg_lo, bd)

    tm_r = min(TM_ROUTE, t)
    wr = jnp.zeros((d, ROUTE_LANES), F32)
    wr = wr.at[:, :N_GROUPS].set(router_group_w[li]).at[:, N_GROUPS:N_GROUPS + N_EXPERTS].set(router_expert_w[li])
    wr_hi, wr_lo = _hi_lo(wr)
    br = jnp.zeros((1, ROUTE_LANES), F32)
    br = br.at[0, :N_GROUPS].set(router_group_b[li]).at[0, N_GROUPS:N_GROUPS + N_EXPERTS].set(router_expert_b[li])
    ii = jnp.arange(tm_r)
    ltri = (ii[None, :] < ii[:, None]).astype(BF16)
    h1, u2, route, route_t, counts = _out_router(mixed.reshape(t, d), x2d, w_out[li].astype(BF16), norm_ffn[li][None, :],
                                        wr_hi, wr_lo, br, ltri)

    n_blocks = -(-(2 * t + N_EXPERTS * (BM_MOE - 1)) // BM_MOE)
    n_rows = n_blocks * BM_MOE
    cnt = counts[0, N_GROUPS:N_GROUPS + N_EXPERTS].astype(jnp.int32)
    padded = (cnt + BM_MOE - 1) // BM_MOE * BM_MOE
    pad_end = jnp.cumsum(padded)
    pad_start = pad_end - padded
    n_used = (pad_end[-1:] // BM_MOE).astype(jnp.int32)
    blk_start = jnp.arange(n_blocks, dtype=jnp.int32) * BM_MOE
    block_expert = jnp.minimum(jnp.sum(pad_end[None, :] <= blk_start[:, None], axis=1), N_EXPERTS - 1).astype(jnp.int32)
    eids = route_t[0:2].astype(jnp.int32)
    ranks = route_t[4:6].astype(jnp.int32)
    sel = eids[:, :, None] == jnp.arange(N_EXPERTS, dtype=jnp.int32)[None, None, :]
    dest = jnp.sum(jnp.where(sel, pad_start[None, None, :], 0), axis=-1) + ranks

    def tiled(tm):
        return dest.reshape(2, t // tm, tm).transpose(1, 0, 2)

    x_rows = _dispatch(pad_end.astype(jnp.int32), padded.astype(jnp.int32), tiled(tm_r), u2, n_rows)
    y_rows = _expert_ffn(block_expert, n_used, x_rows, moe_w_gate[li], moe_w_up[li], moe_w_down[li])
    out = _combine(tiled(min(TM_COMB, t)), h1, route, norm_final[None, :], y_rows)
    return out.reshape(bsz, seq, d)
```

```python
import functools
import math

import jax
import jax.numpy as jnp
from jax import lax
from jax.experimental import pallas as pl
from jax.experimental.pallas import tpu as pltpu

F32 = jnp.float32
BF16 = jnp.bfloat16

D_MODEL = 1024
CHUNK = 64
LOG2_CHUNK = 6
N_META = 16
D_A = 512
HEAD_A = 64
LORA_W = 64
LORA_A = 64
LORA_G = 128
DECAY_SCALE = math.exp(-0.5)
GN_EPS_A = 64e-5
D_B = 512
H_B = 4
HEAD_B = 128
ROPE_BASE = 10000.0
GN_EPS_B = 1e-5
P_A = 3 * D_A + LORA_W + LORA_A + LORA_G
P_B = 4 * D_B
P_IN = P_A + P_B
N_GROUPS = 4
EXPERTS_PER_GROUP = 8
N_EXPERTS = N_GROUPS * EXPERTS_PER_GROUP
D_EXPERT = 512
NORM_EPS = 1e-6

LANES = 128
N_PAIRS = D_A // LANES
ROW_TILES = D_MODEL // LANES
HEAD_SUM_W = 256
VMEM_LIMIT = 48 * 1024 * 1024

ROWS_MIX = 8
GROUP_ROWS = 2
MIXER_PHASES = 3
TM_PROJ = 512
TM_ROUTE = 512
TM_COMB = 256
BM_MOE = 512
ROUTER_ROWS = 40
ROUTE_ROWS = 8
DMA_UNROLL = 8


def _dot(a, b):
    return jnp.dot(a, b, preferred_element_type=F32)


def _dot_nt(a, b):
    return lax.dot_general(a, b, (((1,), (1,)), ((), ())), preferred_element_type=F32)


def _dot_tn(a, b):
    return lax.dot_general(a, b, (((0,), (0,)), ((), ())), preferred_element_type=F32)


def _split2(x):
    hi = x.astype(BF16)
    lo = (x - hi.astype(F32)).astype(BF16)
    return hi, lo


def _split3(x):
    x1 = x.astype(BF16)
    r1 = x - x1.astype(F32)
    x2 = r1.astype(BF16)
    x3 = (r1 - x2.astype(F32)).astype(BF16)
    return x1, x2, x3


def _dot_x2(x, w_exact):
    hi, lo = _split2(x)
    return _dot(hi, w_exact) + _dot(lo, w_exact)


def _dot_3pass(x, w_hi, w_lo):
    hi, lo = _split2(x)
    return _dot(hi, w_hi) + _dot(lo, w_hi) + _dot(hi, w_lo)


def _rms(x, g):
    return x * lax.rsqrt(jnp.mean(x * x, axis=-1, keepdims=True) + NORM_EPS) * g


def _store_rows(ref, x):
    n = ref.shape[0] // ROW_TILES
    for c in range(ROW_TILES):
        ref[pl.ds(c, n, stride=ROW_TILES), :] = x[:, c * LANES:(c + 1) * LANES]


def _load_rows(ref):
    n = ref.shape[0] // ROW_TILES
    return jnp.concatenate([ref[pl.ds(c, n, stride=ROW_TILES), :] for c in range(ROW_TILES)], axis=1)


def _row_tile(ref, row):
    return ref.at[pl.ds(pl.multiple_of(row * ROW_TILES, ROW_TILES), ROW_TILES), :]


def _in_proj_kernel(x_ref, g_ref, w_ref, z_ref):
    u = _rms(x_ref[...], g_ref[...])
    z_ref[...] = _dot(u.astype(BF16), w_ref[...])


def _in_proj(x2d, g, w_bf16):
    m, d = x2d.shape
    tm = min(TM_PROJ, m)
    n = w_bf16.shape[1]
    return pl.pallas_call(
        _in_proj_kernel,
        grid=(m // tm,),
        in_specs=[
            pl.BlockSpec((tm, d), lambda i: (i, 0)),
            pl.BlockSpec((1, d), lambda i: (0, 0)),
            pl.BlockSpec((d, n), lambda i: (0, 0)),
        ],
        out_specs=pl.BlockSpec((tm, n), lambda i: (i, 0)),
        out_shape=jax.ShapeDtypeStruct((m, n), F32),
        compiler_params=pltpu.CompilerParams(
            dimension_semantics=("arbitrary",), vmem_limit_bytes=VMEM_LIMIT),
        name="in_proj",
    )(x2d, g, w_bf16)


def _mixer_kernel(z_ref, zprev0_ref, sa0_ref, rb0_ref, cos_ref, sin_ref, mu_ref, pvec_ref,
                  wwa_hi_ref, wwa_lo_ref, wg_hi_ref, wg_lo_ref, bd_ref,
                  mixed_ref, sa_out_ref, rb_out_ref,
                  sa_scr, rb_scr, zlast_scr, *, rows):
    c_idx = pl.program_id(1)
    n_chunks = pl.num_programs(1)

    @pl.when(c_idx == 0)
    def _():
        for rr in range(rows):
            sa_scr[rr] = sa0_ref[...]
            rb_scr[rr] = rb0_ref[...]
            zlast_scr[rr] = zprev0_ref[...]

    groups = [list(range(g0, min(g0 + GROUP_ROWS, rows))) for g0 in range(0, rows, GROUP_ROWS)]
    phases = [_mixer_group(z_ref, cos_ref, sin_ref, mu_ref, pvec_ref, wwa_hi_ref, wwa_lo_ref, wg_hi_ref,
                           wg_lo_ref, bd_ref, mixed_ref, sa_scr, rb_scr, zlast_scr, grp) for grp in groups]
    for _ in range(MIXER_PHASES):
        for ph in phases:
            next(ph)

    @pl.when(c_idx == n_chunks - 1)
    def _():
        sa_out_ref[...] = sa_scr[...]
        rb_out_ref[...] = rb_scr[...]


def _mixer_group(z_ref, cos_ref, sin_ref, mu_ref, pvec_ref, wwa_hi_ref, wwa_lo_ref, wg_hi_ref, wg_lo_ref,
                 bd_ref, mixed_ref, sa_scr, rb_scr, zlast_scr, grp):
    C = CHUNK
    R = len(grp)
    M = R * C
    rws = [slice(rr * C, (rr + 1) * C) for rr in range(R)]

    z = jnp.concatenate([z_ref[grp[rr]] for rr in range(R)], axis=0)
    za = z[:, :P_A]
    row = lax.broadcasted_iota(jnp.int32, (M, 1), 0)
    zprev = pltpu.roll(za, 1, 0)
    for rr in range(R):
        zprev = jnp.where(row == rr * C, zlast_scr[grp[rr]], zprev)
        zlast_scr[grp[rr]] = za[(rr + 1) * C - 1:(rr + 1) * C, :]
    zs = za + (zprev - za) * mu_ref[...]

    r = zs[:, 0:D_A]
    k = zs[:, D_A:2 * D_A]
    v = zs[:, 2 * D_A:3 * D_A]
    wa = zs[:, 3 * D_A:3 * D_A + LORA_W + LORA_A]
    g_lo = zs[:, 3 * D_A + LORA_W + LORA_A:P_A]

    decay_w0 = pvec_ref[0:1, :]
    iclr_a0 = pvec_ref[1:2, :]
    k_k = pvec_ref[2:3, :]
    k_a = pvec_ref[3:4, :]
    r_k = pvec_ref[4:5, :]
    ln_w = pvec_ref[5:6, :]
    ln_b = pvec_ref[6:7, :]
    gn_w = pvec_ref[7:8, :]

    lane = lax.broadcasted_iota(jnp.int32, (1, LANES), 1)
    wa_act = jnp.where(lane < LORA_W, jnp.tanh(wa), wa)
    pre = _dot_3pass(wa_act, wwa_hi_ref[...], wwa_lo_ref[...])
    log_w = -DECAY_SCALE * jax.nn.sigmoid(decay_w0 + pre[:, :D_A])
    a = jax.nn.sigmoid(iclr_a0 + pre[:, D_A:])
    g = _dot_3pass(jax.nn.sigmoid(g_lo), wg_hi_ref[...], wg_lo_ref[...])

    bd = bd_ref[...]

    def head_sum(t):
        return jnp.concatenate([_dot(t[:, j * HEAD_SUM_W:(j + 1) * HEAD_SUM_W], bd)
                                for j in range(D_A // HEAD_SUM_W)], axis=1)

    kk = k * k_k
    kk = kk / jnp.maximum(jnp.sqrt(head_sum((kk * kk).astype(BF16))), 1e-12)
    k2 = k * (1.0 + (a - 1.0) * k_a)
    b = kk * a

    tm_i = lax.broadcasted_iota(jnp.int32, (M, M), 0)
    sm_j = lax.broadcasted_iota(jnp.int32, (M, M), 1)
    same_row = lax.shift_right_logical(tm_i, LOG2_CHUNK) == lax.shift_right_logical(sm_j, LOG2_CHUNK)
    tril = ((sm_j <= tm_i) & same_row).astype(BF16)
    lw_hi, lw_lo = _split2(log_w)
    cum = _dot(tril, lw_hi) + _dot(tril, lw_lo)
    cmid_r = [cum[rr * C + C // 2 - 1:rr * C + C // 2, :] for rr in range(R)]
    cc = cum - jnp.concatenate([jnp.broadcast_to(cm, (C, D_A)) for cm in cmid_r], axis=0)
    e_nc = jnp.exp(-cc)
    rg = r * jnp.exp(cc)
    kkg = kk * jnp.exp(cc - log_w)
    kinv = k2 * e_nc
    binv = b * e_nc
    e_mid = [jnp.exp(cm) for cm in cmid_r]
    e_end = [jnp.exp(cc[(rr + 1) * C - 1:(rr + 1) * C, :]) for rr in range(R)]
    gam = [jnp.exp(cum[(rr + 1) * C - 1:(rr + 1) * C, :]) for rr in range(R)]

    i2 = lax.broadcasted_iota(jnp.int32, (2 * C, 2 * C), 0)
    j2 = lax.broadcasted_iota(jnp.int32, (2 * C, 2 * C), 1)
    bi = i2 >= C
    bj = j2 >= C
    t2 = jnp.where(bi, i2 - C, i2)
    s2 = jnp.where(bj, j2 - C, j2)
    same_blk = bi == bj
    strict_same = (s2 < t2) & same_blk
    strict_cross = (s2 < t2) & jnp.logical_not(same_blk)
    ta = lax.broadcasted_iota(jnp.int32, (C, 4 * C), 0)
    ja = lax.broadcasted_iota(jnp.int32, (C, 4 * C), 1)
    incl = (ja & (C - 1)) <= ta
    m0 = lane < HEAD_A
    m1 = jnp.logical_not(m0)

    yield

    items =[(rr, p) for rr in range(R) for p in range(N_PAIRS)]
    n_it = range(len(items))
    rsl = [rws[rr] for rr, _ in items]
    lsl = [slice(p * LANES, (p + 1) * LANES) for _, p in items]
    kkg0 = [jnp.where(m0, kkg[rsl[i], lsl[i]], 0.0) for i in n_it]
    kkg1 = [jnp.where(m1, kkg[rsl[i], lsl[i]], 0.0) for i in n_it]
    kinv_i = [kinv[rsl[i], lsl[i]] for i in n_it]
    binv_i = [binv[rsl[i], lsl[i]] for i in n_it]
    v_i = [v[rsl[i], lsl[i]] for i in n_it]
    lhs = [jnp.concatenate([kkg[rsl[i], lsl[i]], rg[rsl[i], lsl[i]]], axis=0).astype(BF16) for i in n_it]
    rhs = [jnp.concatenate([jnp.where(m0, binv_i[i], 0.0), jnp.where(m0, kinv_i[i], 0.0),
                            jnp.where(m1, kinv_i[i], 0.0), jnp.where(m1, binv_i[i], 0.0)], axis=0).astype(BF16)
           for i in n_it]
    out = [_dot_nt(lhs[i], rhs[i]) for i in n_it]
    top = [jnp.concatenate([out[i][:C, :2 * C], out[i][:C, 2 * C:]], axis=0) for i in n_it]
    npow = [jnp.where(strict_same, -top[i], 0.0) for i in n_it]
    q_anti = [jnp.where(strict_cross, top[i], 0.0) for i in n_it]
    vm0 = [jnp.where(m0, v_i[i], 0.0) for i in n_it]
    vm1 = [jnp.where(m1, v_i[i], 0.0) for i in n_it]
    qv = [_dot(q_anti[i].astype(BF16), jnp.concatenate([vm1[i], vm0[i]], axis=0).astype(BF16)) for i in n_it]
    x = [qv[i] + pltpu.roll(jnp.concatenate([kkg0[i], kkg1[i]], axis=0), HEAD_A, 1) for i in n_it]
    for it in range(6):
        nb = [npow[i].astype(BF16) for i in n_it]
        if it < 5:
            prod = [_dot(nb[i], jnp.concatenate([x[i].astype(BF16), nb[i]], axis=1)) for i in n_it]
            x = [x[i] + prod[i][:, :LANES] for i in n_it]
            npow = [prod[i][:, LANES:] for i in n_it]
        else:
            x = [x[i] + _dot(nb[i], x[i].astype(BF16)) for i in n_it]
    w_stack = [jnp.where(same_blk, x[i], 0.0) for i in n_it]
    kkt_stack = [pltpu.roll(jnp.where(same_blk, 0.0, x[i]), HEAD_A, 1) for i in n_it]
    s_old = [sa_scr[grp[rr], p] for rr, p in items]
    s0m = [(s_old[i] * e_mid[items[i][0]][:, lsl[i]]).astype(BF16) for i in n_it]
    u_stack = [_dot_nt(kkt_stack[i].astype(BF16), s0m[i]) + w_stack[i] for i in n_it]
    y_it = [_dot_nt(rg[rsl[i], lsl[i]].astype(BF16), s0m[i]) for i in n_it]
    a_cat = [jnp.where(incl, out[i][C:], 0.0).astype(BF16) for i in n_it]
    y_it = [y_it[i] + _dot(a_cat[i], jnp.concatenate([-u_stack[i][:C], vm0[i], vm1[i], -u_stack[i][C:]],
                                                     axis=0).astype(BF16)) for i in n_it]
    gt = [_dot_tn(jnp.concatenate([v_i[i], -(u_stack[i][:C] + u_stack[i][C:])], axis=0).astype(BF16),
                  jnp.concatenate([kinv_i[i], binv_i[i]], axis=0).astype(BF16)) for i in n_it]
    for i in n_it:
        rr, p = items[i]
        sa_scr[grp[rr], p] = s_old[i] * gam[rr][:, lsl[i]] + jnp.where(same_blk, gt[i], 0.0) * e_end[rr][:, lsl[i]]

    yield

    y = jnp.concatenate([jnp.concatenate(y_it[rr * N_PAIRS:(rr + 1) * N_PAIRS], axis=1) for rr in range(R)],
                        axis=0)
    inv_n = 1.0 / HEAD_A
    mean = head_sum(y.astype(BF16)) * inv_n
    dlt = y - mean
    var = head_sum((dlt * dlt).astype(BF16)) * inv_n
    yn = dlt * lax.rsqrt(var + GN_EPS_A)
    bonus = head_sum((r * k2 * r_k).astype(BF16)) * v
    out_a = (yn * ln_w + ln_b + bonus) * g

    zb = z[:, P_A:]
    cosf = cos_ref[...]
    sinf = sin_ref[...]
    ti = lax.broadcasted_iota(jnp.int32, (C, C), 0)
    sj = lax.broadcasted_iota(jnp.int32, (C, C), 1)
    relf = (ti - sj).astype(F32)
    causal = sj <= ti
    rowf = lax.broadcasted_iota(jnp.int32, (C, 1), 0).astype(F32)
    hb = range(H_B)
    lgs = [math.log1p(-(2.0 ** (-5.0 - h))) for h in hb]
    decay_in = [jnp.where(causal, jnp.exp(lg * jnp.maximum(relf, 0.0)), 0.0) for lg in lgs]
    q_dec = [jnp.exp(lg * (rowf + 1.0)) for lg in lgs]
    k_dec = [jnp.exp(lg * (float(C - 1) - rowf)) for lg in lgs]
    bitems = [(rr, h) for rr in range(R) for h in hb]
    n_b = range(len(bitems))

    def zcol(i, part):
        rr, h = bitems[i]
        return zb[rws[rr], part * D_B + h * HEAD_B:part * D_B + (h + 1) * HEAD_B]

    def rope(t):
        return t * cosf + pltpu.roll(t, HEAD_B // 2, 1) * sinf

    q_b = [rope(zcol(i, 0)).astype(BF16) for i in n_b]
    k_r = [rope(zcol(i, 1)) * (HEAD_B ** -0.5) for i in n_b]
    v_b = [zcol(i, 2).astype(BF16) for i in n_b]
    r_old = [rb_scr[grp[rr], h] for rr, h in bitems]
    qrk = [_dot_nt(q_b[i], jnp.concatenate([r_old[i].astype(BF16), k_r[i].astype(BF16)], axis=0)) for i in n_b]
    cross = [qrk[i][:, :HEAD_B] * q_dec[bitems[i][1]] for i in n_b]
    scores = [qrk[i][:, HEAD_B:] * decay_in[bitems[i][1]] for i in n_b]
    inner = [_dot(scores[i].astype(BF16), v_b[i]) for i in n_b]
    kv = [_dot_tn(v_b[i], (k_r[i] * k_dec[bitems[i][1]]).astype(BF16)) for i in n_b]
    outs_b = []
    for i in n_b:
        rr, h = bitems[i]
        rb_scr[grp[rr], h] = r_old[i] * math.exp(lgs[h] * C) + kv[i]
        y_h = inner[i] + cross[i]
        mu_h = jnp.mean(y_h, axis=-1, keepdims=True)
        d_h = y_h - mu_h
        var_h = jnp.mean(d_h * d_h, axis=-1, keepdims=True)
        yn_h = d_h * lax.rsqrt(var_h + GN_EPS_B)
        g_h = zcol(i, 3)
        outs_b.append(yn_h * gn_w[:, h * HEAD_B:(h + 1) * HEAD_B] * (g_h * jax.nn.sigmoid(g_h)))
    out_b = jnp.concatenate([jnp.concatenate(outs_b[rr * H_B:(rr + 1) * H_B], axis=1) for rr in range(R)], axis=0)

    mixed = jnp.concatenate([out_a, out_b], axis=1).astype(mixed_ref.dtype)
    for rr in range(R):
        mixed_ref[grp[rr]] = mixed[rws[rr]]

    yield


def _mixer(z3, zprev0, sa0, rb0, cosf, sinf, mu, pvec, wwa_hi, wwa_lo, wg_hi, wg_lo, bd):
    bsz, length, _ = z3.shape
    n_chunks = length // CHUNK
    rows = ROWS_MIX if bsz % ROWS_MIX == 0 else 1
    const2 = lambda b, c: (0, 0)
    const3 = lambda b, c: (0, 0, 0)
    st_shape = (N_PAIRS, LANES, LANES)
    return pl.pallas_call(
        functools.partial(_mixer_kernel, rows=rows),
        grid=(bsz // rows, n_chunks),
        in_specs=[
            pl.BlockSpec((rows, CHUNK, P_IN), lambda b, c: (b, c, 0)),
            pl.BlockSpec((1, P_A), const2),
            pl.BlockSpec(st_shape, const3),
            pl.BlockSpec((H_B, HEAD_B, HEAD_B), const3),
            pl.BlockSpec((CHUNK, HEAD_B), lambda b, c: (c, 0)),
            pl.BlockSpec((CHUNK, HEAD_B), lambda b, c: (c, 0)),
            pl.BlockSpec((1, P_A), const2),
            pl.BlockSpec((8, D_A), const2),
            pl.BlockSpec((LORA_W + LORA_A, 2 * D_A), const2),
            pl.BlockSpec((LORA_W + LORA_A, 2 * D_A), const2),
            pl.BlockSpec((LORA_G, D_A), const2),
            pl.BlockSpec((LORA_G, D_A), const2),
            pl.BlockSpec((HEAD_SUM_W, HEAD_SUM_W), const2),
        ],
        out_specs=[
            pl.BlockSpec((rows, CHUNK, D_A + D_B), lambda b, c: (b, c, 0)),
            pl.BlockSpec((rows,) + st_shape, lambda b, c: (b, 0, 0, 0)),
            pl.BlockSpec((rows, H_B, HEAD_B, HEAD_B), lambda b, c: (b, 0, 0, 0)),
        ],
        out_shape=[
            jax.ShapeDtypeStruct((bsz, length, D_A + D_B), BF16),
            jax.ShapeDtypeStruct((bsz,) + st_shape, F32),
            jax.ShapeDtypeStruct((bsz, H_B, HEAD_B, HEAD_B), F32),
        ],
        scratch_shapes=[
            pltpu.VMEM((rows,) + st_shape, F32),
            pltpu.VMEM((rows, H_B, HEAD_B, HEAD_B), F32),
            pltpu.VMEM((rows, 1, P_A), F32),
        ],
        compiler_params=pltpu.CompilerParams(
            dimension_semantics=("arbitrary", "arbitrary"), vmem_limit_bytes=VMEM_LIMIT),
        name="mixer",
    )(z3, zprev0, sa0, rb0, cosf, sinf, mu, pvec, wwa_hi, wwa_lo, wg_hi, wg_lo, bd)


def _out_router_kernel(mixed_ref, x_ref, wout_ref, g_ref, wrt_hi_ref, wrt_lo_ref, brt_ref, utri_ref,
                       h1_ref, u2_ref, route_t_ref, counts_ref, cnt_scr):
    i = pl.program_id(0)

    @pl.when(i == 0)
    def _():
        cnt_scr[...] = jnp.zeros_like(cnt_scr)

    h1 = x_ref[...] + _dot(mixed_ref[...], wout_ref[...])
    h1_ref[...] = h1
    u2 = _rms(h1, g_ref[...])
    _store_rows(u2_ref, u2)

    u_hi, u_lo = _split2(u2)
    wrt_hi = wrt_hi_ref[...]
    logits = (_dot_nt(wrt_hi, u_hi) + _dot_nt(wrt_hi, u_lo) + _dot_nt(wrt_lo_ref[...], u_hi)
              + brt_ref[...][:, 0:1])

    tm = logits.shape[1]
    rowi = lax.broadcasted_iota(jnp.int32, (ROUTER_ROWS, tm), 0).astype(F32)
    neg = -jnp.inf
    big = float(ROUTER_ROWS)
    first = float(N_GROUPS)

    def rmax(t):
        return jnp.max(t, axis=0, keepdims=True)

    def rsum(t):
        return jnp.sum(t, axis=0, keepdims=True)

    def rmin(t):
        return jnp.min(t, axis=0, keepdims=True)

    gmask = rowi < N_GROUPS
    gmax = rmax(jnp.where(gmask, logits, neg))
    gexp = jnp.where(gmask, jnp.exp(logits - gmax), 0.0)
    gprob = gexp / rsum(gexp)
    g_p = rmax(gprob)
    g_idx = rmin(jnp.where(gmask & (gprob == g_p), rowi, big))

    lo_row = first + EXPERTS_PER_GROUP * g_idx
    emask = (rowi >= lo_row) & (rowi < lo_row + EXPERTS_PER_GROUP)
    emax = rmax(jnp.where(emask, logits, neg))
    eexp = jnp.where(emask, jnp.exp(logits - emax), 0.0)
    eprob = jnp.where(emask, eexp / rsum(eexp), -1.0)
    p1 = rmax(eprob)
    i1 = rmin(jnp.where(eprob == p1, rowi, big))
    eprob2 = jnp.where(rowi == i1, -1.0, eprob)
    p2 = rmax(eprob2)
    i2 = rmin(jnp.where(eprob2 == p2, rowi, big))
    w1 = g_p * p1 / (p1 + p2)
    w2 = g_p * p2 / (p1 + p2)

    sel1 = rowi == i1
    sel2 = rowi == i2
    onehot = jnp.where(sel1 | sel2, 1.0, 0.0)
    before = _dot(onehot.astype(BF16), utri_ref[...]) + cnt_scr[...][:, 0:1]
    rank1 = rsum(jnp.where(sel1, before, 0.0))
    rank2 = rsum(jnp.where(sel2, before, 0.0))
    cnt_scr[...] = cnt_scr[...] + jnp.sum(onehot, axis=1, keepdims=True)
    counts_ref[...] = cnt_scr[...]

    rr = lax.broadcasted_iota(jnp.int32, (ROUTE_ROWS, tm), 0)
    rec = jnp.where(rr == 0, i1 - first, 0.0)
    rec = jnp.where(rr == 1, i2 - first, rec)
    rec = jnp.where(rr == 2, w1, rec)
    rec = jnp.where(rr == 3, w2, rec)
    rec = jnp.where(rr == 4, rank1, rec)
    rec = jnp.where(rr == 5, rank2, rec)
    route_t_ref[...] = rec


def _out_router(mixed2d, x2d, wout_bf16, g, wrt_hi, wrt_lo, brt, utri):
    t, d = x2d.shape
    tm = utri.shape[0]
    c2 = lambda i: (0, 0)
    return pl.pallas_call(
        _out_router_kernel,
        grid=(t // tm,),
        in_specs=[
            pl.BlockSpec((tm, d), lambda i: (i, 0)),
            pl.BlockSpec((tm, d), lambda i: (i, 0)),
            pl.BlockSpec((d, d), c2),
            pl.BlockSpec((1, d), c2),
            pl.BlockSpec((ROUTER_ROWS, d), c2),
            pl.BlockSpec((ROUTER_ROWS, d), c2),
            pl.BlockSpec((ROUTER_ROWS, LANES), c2),
            pl.BlockSpec((tm, tm), c2),
        ],
        out_specs=[
            pl.BlockSpec((tm, d), lambda i: (i, 0)),
            pl.BlockSpec((tm * ROW_TILES, LANES), lambda i: (i, 0)),
            pl.BlockSpec((ROUTE_ROWS, tm), lambda i: (0, i)),
            pl.BlockSpec((ROUTER_ROWS, LANES), c2),
        ],
        out_shape=[
            jax.ShapeDtypeStruct((t, d), F32),
            jax.ShapeDtypeStruct((t * ROW_TILES, LANES), F32),
            jax.ShapeDtypeStruct((ROUTE_ROWS, t), F32),
            jax.ShapeDtypeStruct((ROUTER_ROWS, LANES), F32),
        ],
        scratch_shapes=[pltpu.VMEM((ROUTER_ROWS, LANES), F32)],
        compiler_params=pltpu.CompilerParams(
            dimension_semantics=("arbitrary",), vmem_limit_bytes=VMEM_LIMIT),
        name="out_router",
    )(mixed2d, x2d, wout_bf16, g, wrt_hi, wrt_lo, brt, utri)


def _row_copy(src_ref, src_row, dst_ref, dst_row, sem):
    return pltpu.make_async_copy(_row_tile(src_ref, src_row), _row_tile(dst_ref, dst_row), sem)


def _dispatch_kernel(pad_end_ref, padded_ref, dest_ref, u2_ref, xrows_ref, zero_scr, zsem, sem):
    i = pl.program_id(0)
    tm = u2_ref.shape[0] // ROW_TILES
    blk = BM_MOE * ROW_TILES
    n_blocks = xrows_ref.shape[0] // blk
    n_used = pad_end_ref[N_EXPERTS - 1] // BM_MOE

    def zero_block(start):
        return pltpu.make_async_copy(
            zero_scr, xrows_ref.at[pl.ds(pl.multiple_of(start * ROW_TILES, blk), blk), :], zsem)

    def zero_copy(e):
        return zero_block(pad_end_ref[e] - BM_MOE)

    @pl.when(i == 0)
    def _():
        zero_scr[...] = jnp.zeros_like(zero_scr)
        for e in range(N_EXPERTS):
            @pl.when(padded_ref[e] > 0)
            def _():
                zero_copy(e).start()

        def tail_start(j, carry):
            zero_block(j * BM_MOE).start()
            return carry

        lax.fori_loop(n_used, n_blocks, tail_start, 0)
        for e in range(N_EXPERTS):
            @pl.when(padded_ref[e] > 0)
            def _():
                zero_copy(e).wait()

        def tail_wait(j, carry):
            zero_block(j * BM_MOE).wait()
            return carry

        lax.fori_loop(n_used, n_blocks, tail_wait, 0)

    def issue(jj, carry):
        for u in range(DMA_UNROLL):
            j = jj * DMA_UNROLL + u
            _row_copy(u2_ref, j, xrows_ref, dest_ref[0, 0, j], sem).start(priority=0)
            _row_copy(u2_ref, j, xrows_ref, dest_ref[0, 1, j], sem).start(priority=1)
        return carry

    lax.fori_loop(0, tm // DMA_UNROLL, issue, 0)

    def drain(j, carry):
        _row_copy(u2_ref, j, xrows_ref, dest_ref[0, 0, j], sem).wait()
        _row_copy(u2_ref, j, xrows_ref, dest_ref[0, 1, j], sem).wait()
        return carry

    lax.fori_loop(0, tm, drain, 0, unroll=DMA_UNROLL)


def _dispatch(pad_end, padded, dest3, u2, n_rows):
    t = u2.shape[0] // ROW_TILES
    tm = dest3.shape[2]
    grid_spec = pltpu.PrefetchScalarGridSpec(
        num_scalar_prefetch=2,
        grid=(t // tm,),
        in_specs=[
            pl.BlockSpec((1, 2, tm), lambda i, pe, pd: (i, 0, 0), memory_space=pltpu.SMEM),
            pl.BlockSpec((tm * ROW_TILES, LANES), lambda i, pe, pd: (i, 0)),
        ],
        out_specs=pl.BlockSpec(memory_space=pl.ANY),
        scratch_shapes=[
            pltpu.VMEM((BM_MOE * ROW_TILES, LANES), F32),
            pltpu.SemaphoreType.DMA(()),
            pltpu.SemaphoreType.DMA(()),
        ],
    )
    return pl.pallas_call(
        _dispatch_kernel,
        grid_spec=grid_spec,
        out_shape=jax.ShapeDtypeStruct((n_rows * ROW_TILES, LANES), F32),
        compiler_params=pltpu.CompilerParams(
            dimension_semantics=("arbitrary",), vmem_limit_bytes=VMEM_LIMIT),
        name="dispatch",
    )(pad_end, padded, dest3, u2)


def _expert_ffn_kernel(be_ref, nused_ref, x_ref, wg_ref, wu_ref, wd_ref, y_ref, wg_s, wu_s, wd_s):
    i = pl.program_id(0)
    used = i < nused_ref[0]

    @pl.when(used & ((i == 0) | (be_ref[i] != be_ref[jnp.maximum(i - 1, 0)])))
    def _():
        wg_s[...] = wg_ref[0].astype(BF16)
        wu_s[...] = wu_ref[0].astype(BF16)
        wd_s[...] = wd_ref[0].astype(BF16)

    @pl.when(used)
    def _():
        x = _load_rows(x_ref).astype(BF16)
        hg = _dot(x, wg_s[...])
        hu = _dot(x, wu_s[...])
        hid = (hg * jax.nn.sigmoid(hg)) * hu
        _store_rows(y_ref, _dot(hid.astype(BF16), wd_s[...]))

    @pl.when(jnp.logical_not(used))
    def _():
        y_ref[...] = jnp.zeros_like(y_ref)


def _expert_ffn(block_expert, n_used, x_rows, wg, wu, wd):
    blk = BM_MOE * ROW_TILES
    n_blocks = x_rows.shape[0] // blk
    d, de = wg.shape[1], wg.shape[2]

    def row_map(i, be, nu):
        return (jnp.minimum(i, nu[0] - 1), 0)

    def w_map(i, be, nu):
        return (be[jnp.minimum(i, nu[0] - 1)], 0, 0)

    grid_spec = pltpu.PrefetchScalarGridSpec(
        num_scalar_prefetch=2,
        grid=(n_blocks,),
        in_specs=[
            pl.BlockSpec((blk, LANES), row_map),
            pl.BlockSpec((1, d, de), w_map),
            pl.BlockSpec((1, d, de), w_map),
            pl.BlockSpec((1, de, d), w_map),
        ],
        out_specs=pl.BlockSpec((blk, LANES), lambda i, be, nu: (i, 0)),
        scratch_shapes=[pltpu.VMEM((d, de), BF16), pltpu.VMEM((d, de), BF16), pltpu.VMEM((de, d), BF16)],
    )
    return pl.pallas_call(
        _expert_ffn_kernel,
        grid_spec=grid_spec,
        out_shape=jax.ShapeDtypeStruct(x_rows.shape, F32),
        compiler_params=pltpu.CompilerParams(
            dimension_semantics=("arbitrary",), vmem_limit_bytes=VMEM_LIMIT),
        name="expert_ffn",
    )(block_expert, n_used, x_rows, wg, wu, wd)


def _combine_kernel(dest_ref, dnext_ref, h1_ref, route_ref, g_ref, yrows_ref, out_ref, ybuf, sems):
    i = pl.program_id(0)
    n = pl.num_programs(0)
    tm = h1_ref.shape[0]
    slot = lax.rem(i, 2)

    def copies(dref, s, j):
        return (_row_copy(yrows_ref, dref[0, 0, j], ybuf.at[s, 0], j, sems.at[s]),
                _row_copy(yrows_ref, dref[0, 1, j], ybuf.at[s, 1], j, sems.at[s]))

    def issue_tile(dref, s):
        def body(jj, carry):
            for u in range(DMA_UNROLL):
                c0, c1 = copies(dref, s, jj * DMA_UNROLL + u)
                c0.start(priority=0)
                c1.start(priority=1)
            return carry

        lax.fori_loop(0, tm // DMA_UNROLL, body, 0)

    @pl.when(i == 0)
    def _():
        issue_tile(dest_ref, 0)

    @pl.when(i + 1 < n)
    def _():
        issue_tile(dnext_ref, 1 - slot)

    def drain(j, carry):
        c0, c1 = copies(dest_ref, slot, j)
        c0.wait()
        c1.wait()
        return carry

    lax.fori_loop(0, tm, drain, 0, unroll=DMA_UNROLL)

    rec = jnp.concatenate([route_ref[...], jnp.zeros((LANES - ROUTE_ROWS, tm), F32)], axis=0).T
    w1 = rec[:, 2:3]
    w2 = rec[:, 3:4]
    h =h1_ref[...] + _load_rows(ybuf.at[slot, 0]) * w1 + _load_rows(ybuf.at[slot, 1]) * w2
    out_ref[...] = _rms(h, g_ref[...])


def _combine(dest3, h1, route, g, y_rows):
    t, d = h1.shape
    tm = dest3.shape[2]
    n = t // tm
    return pl.pallas_call(
        _combine_kernel,
        grid=(n,),
        in_specs=[
            pl.BlockSpec((1, 2, tm), lambda i: (i, 0, 0), memory_space=pltpu.SMEM),
            pl.BlockSpec((1, 2, tm), lambda i: (jnp.minimum(i + 1, n - 1), 0, 0), memory_space=pltpu.SMEM),
            pl.BlockSpec((tm, d), lambda i: (i, 0)),
            pl.BlockSpec((ROUTE_ROWS, tm), lambda i: (0, i)),
            pl.BlockSpec((1, d), lambda i: (0, 0)),
            pl.BlockSpec(memory_space=pl.ANY),
        ],
        out_specs=pl.BlockSpec((tm, d), lambda i: (i, 0)),
        out_shape=jax.ShapeDtypeStruct((t, d), F32),
        scratch_shapes=[pltpu.VMEM((2, 2, tm * ROW_TILES, LANES), F32), pltpu.SemaphoreType.DMA((2,))],
        compiler_params=pltpu.CompilerParams(
            dimension_semantics=("arbitrary",), vmem_limit_bytes=VMEM_LIMIT),
        name="combine",
    )(dest3, dest3, h1, route, g, y_rows)


def _rope_tables(positions):
    inv_freq = ROPE_BASE ** (-jnp.arange(0, HEAD_B, 2, dtype=F32) / HEAD_B)
    ang = positions.astype(F32)[:, None] * inv_freq[None, :]
    cos, sin = jnp.cos(ang), jnp.sin(ang)
    return jnp.concatenate([cos, cos], axis=1), jnp.concatenate([-sin, sin], axis=1)


def _hi_lo(w):
    hi = w.astype(BF16)
    return hi, (w - hi.astype(F32)).astype(BF16)


def kernel(x, meta_tokens, norm_mix, w_in, shift_mu, decay_w0, decay_up, iclr_a0, iclr_up, gate_up, k_k, k_a, r_k, ln_w_a, ln_b_a, gn_w_b, w_out, norm_ffn, router_group_w, router_group_b, router_expert_w, router_expert_b, moe_w_gate, moe_w_up, moe_w_down, norm_final):
    bsz, seq, d = x.shape
    assert d == D_MODEL and seq % CHUNK == 0 and norm_mix.shape[0] == 1
    t = bsz * seq
    li = 0

    w_in_b = w_in[li].astype(BF16)
    g_mix = norm_mix[li][None, :]
    mu = shift_mu[li][None, :]
    pvec = jnp.stack([decay_w0[li], iclr_a0[li], k_k[li], k_a[li], r_k[li], ln_w_a[li], ln_b_a[li], gn_w_b[li]])
    wwa = jnp.zeros((LORA_W + LORA_A, 2 * D_A), F32)
    wwa = wwa.at[:LORA_W, :D_A].set(decay_up[li]).at[LORA_W:, D_A:].set(iclr_up[li])
    wwa_hi, wwa_lo = _hi_lo(wwa)
    wg_hi, wg_lo = _hi_lo(gate_up[li])
    ch = jnp.arange(HEAD_SUM_W) // HEAD_A
    bd = (ch[:, None] == ch[None, :]).astype(BF16)

    meta_pad = jnp.concatenate([jnp.zeros((CHUNK - N_META, d), F32), meta_tokens.astype(F32)], axis=0)
    z_meta = _in_proj(meta_pad, g_mix, w_in_b)
    cos_m, sin_m = _rope_tables(jnp.arange(CHUNK) - (CHUNK - N_META))
    zeros_a = jnp.zeros((N_PAIRS, LANES, LANES), F32)
    zeros_b = jnp.zeros((H_B, HEAD_B, HEAD_B), F32)
    _, sa_meta, rb_meta = _mixer(z_meta[None], jnp.zeros((1, P_A), F32), zeros_a, zeros_b, cos_m, sin_m,
                                 mu, pvec, wwa_hi, wwa_lo, wg_hi, wg_lo, bd)

    x2d = x.reshape(t, d)
    z = _in_proj(x2d, g_mix, w_in_b)
    cos_x, sin_x = _rope_tables(N_META + jnp.arange(seq))
    mixed, _, _ = _mixer(z.reshape(bsz, seq, P_IN), z_meta[CHUNK - 1:CHUNK, :P_A], sa_meta[0], rb_meta[0],
                         cos_x, sin_x, mu, pvec, wwa_hi, wwa_lo, wg_hi, wg_lo, bd)

    tm_r = min(TM_ROUTE, t)
    n_log = N_GROUPS + N_EXPERTS
    wrt = jnp.zeros((ROUTER_ROWS, d), F32)
    wrt = wrt.at[:N_GROUPS].set(router_group_w[li].T).at[N_GROUPS:n_log].set(router_expert_w[li].T)
    wrt_hi, wrt_lo = _hi_lo(wrt)
    brt = jnp.zeros((ROUTER_ROWS,), F32)
    brt = brt.at[:N_GROUPS].set(router_group_b[li]).at[N_GROUPS:n_log].set(router_expert_b[li])
    brt = jnp.broadcast_to(brt[:, None], (ROUTER_ROWS, LANES))
    ii = jnp.arange(tm_r)
    utri = (ii[:, None] < ii[None, :]).astype(BF16)
    h1, u2, route_t, counts = _out_router(mixed.reshape(t, d), x2d, w_out[li].astype(BF16), norm_ffn[li][None, :],
                                          wrt_hi, wrt_lo, brt, utri)

    n_blocks = -(-(2 * t + N_EXPERTS * (BM_MOE - 1)) // BM_MOE)
    n_rows = n_blocks * BM_MOE
    cnt = counts[N_GROUPS:n_log, 0].astype(jnp.int32)
    padded = (cnt + BM_MOE - 1) // BM_MOE * BM_MOE
    pad_end = jnp.cumsum(padded)
    pad_start = pad_end - padded
    n_used = (pad_end[-1:] // BM_MOE).astype(jnp.int32)
    blk_start = jnp.arange(n_blocks, dtype=jnp.int32) * BM_MOE
    block_expert = jnp.minimum(jnp.sum(pad_end[None, :] <= blk_start[:, None], axis=1), N_EXPERTS - 1).astype(jnp.int32)
    eids = route_t[0:2].astype(jnp.int32)
    ranks = route_t[4:6].astype(jnp.int32)
    sel = eids[:, :, None] == jnp.arange(N_EXPERTS, dtype=jnp.int32)[None, None, :]
    dest = jnp.sum(jnp.where(sel, pad_start[None, None, :], 0), axis=-1) + ranks

    def tiled(tm):
        return dest.reshape(2, t // tm, tm).transpose(1, 0, 2)

    x_rows = _dispatch(pad_end.astype(jnp.int32), padded.astype(jnp.int32), tiled(tm_r), u2, n_rows)
    y_rows = _expert_ffn(block_expert, n_used, x_rows, moe_w_gate[li], moe_w_up[li], moe_w_down[li])
    out = _combine(tiled(min(TM_COMB, t)), h1, route_t, norm_final[None, :], y_rows)
    return out.reshape(bsz, seq, d)
```

```python
import functools
import math

import jax
import jax.numpy as jnp
from jax import lax
from jax.experimental import pallas as pl
from jax.experimental.pallas import tpu as pltpu

F32 = jnp.float32
BF16 = jnp.bfloat16

D_MODEL = 1024
CHUNK = 64
LOG2_CHUNK = 6
N_META = 16
D_A = 512
HEAD_A = 64
LORA_W = 64
LORA_A = 64
LORA_G = 128
DECAY_SCALE = math.exp(-0.5)
GN_EPS_A = 64e-5
D_B = 512
H_B = 4
HEAD_B = 128
ROPE_BASE = 10000.0
GN_EPS_B = 1e-5
P_A = 3 * D_A + LORA_W + LORA_A + LORA_G
P_B = 4 * D_B
P_IN = P_A + P_B
N_GROUPS = 4
EXPERTS_PER_GROUP = 8
N_EXPERTS = N_GROUPS * EXPERTS_PER_GROUP
D_EXPERT = 512
NORM_EPS = 1e-6

LANES = 128
N_PAIRS = D_A // LANES
ROW_TILES = D_MODEL // LANES
HEAD_SUM_W = 256
VMEM_LIMIT = 48 * 1024 * 1024

ROWS_MIX = 8
GROUP_ROWS = 2
MIXER_PHASES = 3
TM_PROJ = 512
TM_ROUTE = 512
TM_COMB = 256
BM_MOE = 512
ROUTER_ROWS = 40
ROUTE_ROWS = 8
DMA_UNROLL = 8


def _dot(a, b):
    return jnp.dot(a, b, preferred_element_type=F32)


def _dot_nt(a, b):
    return lax.dot_general(a, b, (((1,), (1,)), ((), ())), preferred_element_type=F32)


def _dot_tn(a, b):
    return lax.dot_general(a, b, (((0,), (0,)), ((), ())), preferred_element_type=F32)


def _split2(x):
    hi = x.astype(BF16)
    lo = (x - hi.astype(F32)).astype(BF16)
    return hi, lo


def _rms(x, g):
    return x * lax.rsqrt(jnp.mean(x * x, axis=-1, keepdims=True) + NORM_EPS) * g


def _store_rows(ref, x):
    n = ref.shape[0] // ROW_TILES
    for c in range(ROW_TILES):
        ref[pl.ds(c, n, stride=ROW_TILES), :] = x[:, c * LANES:(c + 1) * LANES]


def _load_rows(ref):
    n = ref.shape[0] // ROW_TILES
    return jnp.concatenate([ref[pl.ds(c, n, stride=ROW_TILES), :] for c in range(ROW_TILES)], axis=1)


def _row_tile(ref, row):
    return ref.at[pl.ds(pl.multiple_of(row * ROW_TILES, ROW_TILES), ROW_TILES), :]


def _in_proj_kernel(x_ref, g_ref, w_ref, z_ref):
    u = _rms(x_ref[...], g_ref[...])
    z_ref[...] = _dot(u.astype(BF16), w_ref[...])


def _in_proj(x2d, g, w_bf16):
    m, d = x2d.shape
    tm = min(TM_PROJ, m)
    n = w_bf16.shape[1]
    return pl.pallas_call(
        _in_proj_kernel,
        grid=(m // tm,),
        in_specs=[
            pl.BlockSpec((tm, d), lambda i: (i, 0)),
            pl.BlockSpec((1, d), lambda i: (0, 0)),
            pl.BlockSpec((d, n), lambda i: (0, 0)),
        ],
        out_specs=pl.BlockSpec((tm, n), lambda i: (i, 0)),
        out_shape=jax.ShapeDtypeStruct((m, n), F32),
        compiler_params=pltpu.CompilerParams(
            dimension_semantics=("arbitrary",), vmem_limit_bytes=VMEM_LIMIT),
        name="in_proj",
    )(x2d, g, w_bf16)


def _mixer_kernel(z_ref, zprev0_ref, sa0_ref, rb0_ref, cos_ref, sin_ref, mu_ref, pvec_ref,
                  wwa_ref, wg_ref, bd_ref,
                  mixed_ref, sa_out_ref, rb_out_ref,
                  sa_scr, rb_scr, zlast_scr, *, rows):
    c_idx = pl.program_id(1)
    n_chunks = pl.num_programs(1)

    @pl.when(c_idx == 0)
    def _():
        for rr in range(rows):
            sa_scr[rr] = sa0_ref[...]
            rb_scr[rr] = rb0_ref[...]
            zlast_scr[rr] = zprev0_ref[...]

    groups = [list(range(g0, min(g0 + GROUP_ROWS, rows))) for g0 in range(0, rows, GROUP_ROWS)]
    phases = [_mixer_group(z_ref, cos_ref, sin_ref, mu_ref, pvec_ref, wwa_ref, wg_ref, bd_ref,
                           mixed_ref, sa_scr, rb_scr, zlast_scr, grp) for grp in groups]
    for _ in range(MIXER_PHASES):
        for ph in phases:
            next(ph)

    @pl.when(c_idx == n_chunks - 1)
    def _():
        sa_out_ref[...] = sa_scr[...]
        rb_out_ref[...] = rb_scr[...]


def _mixer_group(z_ref, cos_ref, sin_ref, mu_ref, pvec_ref, wwa_ref, wg_ref, bd_ref,
                 mixed_ref, sa_scr, rb_scr, zlast_scr, grp):
    C = CHUNK
    R = len(grp)
    M = R * C
    rws = [slice(rr * C, (rr + 1) * C) for rr in range(R)]

    z = jnp.concatenate([z_ref[grp[rr]] for rr in range(R)], axis=0)
    za = z[:, :P_A]
    row = lax.broadcasted_iota(jnp.int32, (M, 1), 0)
    zprev = pltpu.roll(za, 1, 0)
    for rr in range(R):
        zprev = jnp.where(row == rr * C, zlast_scr[grp[rr]], zprev)
        zlast_scr[grp[rr]] = za[(rr + 1) * C - 1:(rr + 1) * C, :]
    zs = za + (zprev - za) * mu_ref[...]

    r = zs[:, 0:D_A]
    k = zs[:, D_A:2 * D_A]
    v = zs[:, 2 * D_A:3 * D_A]
    wa = zs[:, 3 * D_A:3 * D_A + LORA_W + LORA_A]
    g_lo = zs[:, 3 * D_A + LORA_W + LORA_A:P_A]

    decay_w0 = pvec_ref[0:1, :]
    iclr_a0 = pvec_ref[1:2, :]
    k_k = pvec_ref[2:3, :]
    k_a = pvec_ref[3:4, :]
    r_k = pvec_ref[4:5, :]
    ln_w = pvec_ref[5:6, :]
    ln_b = pvec_ref[6:7, :]
    gn_w = pvec_ref[7:8, :]

    lane = lax.broadcasted_iota(jnp.int32, (1, LANES), 1)
    wa_act = jnp.where(lane < LORA_W, jnp.tanh(wa), wa)
    pre = _dot(wa_act.astype(BF16), wwa_ref[...])
    log_w = -DECAY_SCALE * jax.nn.sigmoid(decay_w0 + pre[:, :D_A])
    a = jax.nn.sigmoid(iclr_a0 + pre[:, D_A:])
    g = _dot(jax.nn.sigmoid(g_lo).astype(BF16), wg_ref[...])

    bd = bd_ref[...]

    def head_sum(t):
        return jnp.concatenate([_dot(t[:, j * HEAD_SUM_W:(j + 1) * HEAD_SUM_W], bd)
                                for j in range(D_A // HEAD_SUM_W)], axis=1)

    kk = k * k_k
    kk = kk / jnp.maximum(jnp.sqrt(head_sum((kk * kk).astype(BF16))), 1e-12)
    k2 = k * (1.0 + (a - 1.0) * k_a)
    b = kk * a

    tm_i = lax.broadcasted_iota(jnp.int32, (M, M), 0)
    sm_j = lax.broadcasted_iota(jnp.int32, (M, M), 1)
    same_row = lax.shift_right_logical(tm_i, LOG2_CHUNK) == lax.shift_right_logical(sm_j, LOG2_CHUNK)
    tril = ((sm_j <= tm_i) & same_row).astype(BF16)
    lw_hi, lw_lo = _split2(log_w)
    cum = _dot(tril, lw_hi) + _dot(tril, lw_lo)
    cmid_r = [cum[rr * C + C // 2 - 1:rr * C + C // 2, :] for rr in range(R)]
    cc = cum - jnp.concatenate([jnp.broadcast_to(cm, (C, D_A)) for cm in cmid_r], axis=0)
    e_nc = jnp.exp(-cc)
    rg = r * jnp.exp(cc)
    kkg = kk * jnp.exp(cc - log_w)
    kinv = k2 * e_nc
    binv = b * e_nc
    e_mid = [jnp.exp(cm) for cm in cmid_r]
    e_end = [jnp.exp(cc[(rr + 1) * C - 1:(rr + 1) * C, :]) for rr in range(R)]
    gam = [jnp.exp(cum[(rr + 1) * C - 1:(rr + 1) * C, :]) for rr in range(R)]

    i2 = lax.broadcasted_iota(jnp.int32, (2 * C, 2 * C), 0)
    j2 = lax.broadcasted_iota(jnp.int32, (2 * C, 2 * C), 1)
    bi = i2 >= C
    bj = j2 >= C
    t2 = jnp.where(bi, i2 - C, i2)
    s2 = jnp.where(bj, j2 - C, j2)
    same_blk = bi == bj
    strict_same = (s2 < t2) & same_blk
    strict_cross = (s2 < t2) & jnp.logical_not(same_blk)
    ta = lax.broadcasted_iota(jnp.int32, (C, 4 * C), 0)
    ja = lax.broadcasted_iota(jnp.int32, (C, 4 * C), 1)
    incl = (ja & (C - 1)) <= ta
    m0 = lane < HEAD_A
    m1 = jnp.logical_not(m0)

    yield

    items =[(rr, p) for rr in range(R) for p in range(N_PAIRS)]
    n_it = range(len(items))
    rsl = [rws[rr] for rr, _ in items]
    lsl = [slice(p * LANES, (p + 1) * LANES) for _, p in items]
    kkg0 = [jnp.where(m0, kkg[rsl[i], lsl[i]], 0.0) for i in n_it]
    kkg1 = [jnp.where(m1, kkg[rsl[i], lsl[i]], 0.0) for i in n_it]
    kinv_i = [kinv[rsl[i], lsl[i]] for i in n_it]
    binv_i = [binv[rsl[i], lsl[i]] for i in n_it]
    v_i = [v[rsl[i], lsl[i]] for i in n_it]
    lhs = [jnp.concatenate([kkg[rsl[i], lsl[i]], rg[rsl[i], lsl[i]]], axis=0).astype(BF16) for i in n_it]
    rhs = [jnp.concatenate([jnp.where(m0, binv_i[i], 0.0), jnp.where(m0, kinv_i[i], 0.0),
                            jnp.where(m1, kinv_i[i], 0.0), jnp.where(m1, binv_i[i], 0.0)], axis=0).astype(BF16)
           for i in n_it]
    out = [_dot_nt(lhs[i], rhs[i]) for i in n_it]
    top = [jnp.concatenate([out[i][:C, :2 * C], out[i][:C, 2 * C:]], axis=0) for i in n_it]
    npow = [jnp.where(strict_same, -top[i], 0.0) for i in n_it]
    q_anti = [jnp.where(strict_cross, top[i], 0.0) for i in n_it]
    vm0 = [jnp.where(m0, v_i[i], 0.0) for i in n_it]
    vm1 = [jnp.where(m1, v_i[i], 0.0) for i in n_it]
    qv = [_dot(q_anti[i].astype(BF16), jnp.concatenate([vm1[i], vm0[i]], axis=0).astype(BF16)) for i in n_it]
    x = [qv[i] + pltpu.roll(jnp.concatenate([kkg0[i], kkg1[i]], axis=0), HEAD_A, 1) for i in n_it]
    for it in range(6):
        nb = [npow[i].astype(BF16) for i in n_it]
        if it < 5:
            prod = [_dot(nb[i], jnp.concatenate([x[i].astype(BF16), nb[i]], axis=1)) for i in n_it]
            x = [x[i] + prod[i][:, :LANES] for i in n_it]
            npow = [prod[i][:, LANES:] for i in n_it]
        else:
            x = [x[i] + _dot(nb[i], x[i].astype(BF16)) for i in n_it]
    w_stack = [jnp.where(same_blk, x[i], 0.0) for i in n_it]
    kkt_stack = [pltpu.roll(jnp.where(same_blk, 0.0, x[i]), HEAD_A, 1) for i in n_it]
    s_old = [sa_scr[grp[rr], p] for rr, p in items]
    s0m = [(s_old[i] * e_mid[items[i][0]][:, lsl[i]]).astype(BF16) for i in n_it]
    u_stack = [_dot_nt(kkt_stack[i].astype(BF16), s0m[i]) + w_stack[i] for i in n_it]
    y_it = [_dot_nt(rg[rsl[i], lsl[i]].astype(BF16), s0m[i]) for i in n_it]
    a_cat = [jnp.where(incl, out[i][C:], 0.0).astype(BF16) for i in n_it]
    y_it = [y_it[i] + _dot(a_cat[i], jnp.concatenate([-u_stack[i][:C], vm0[i], vm1[i], -u_stack[i][C:]],
                                                     axis=0).astype(BF16)) for i in n_it]
    gt = [_dot_tn(jnp.concatenate([v_i[i], -(u_stack[i][:C] + u_stack[i][C:])], axis=0).astype(BF16),
                  jnp.concatenate([kinv_i[i], binv_i[i]], axis=0).astype(BF16)) for i in n_it]
    for i in n_it:
        rr, p = items[i]
        sa_scr[grp[rr], p] = s_old[i] * gam[rr][:, lsl[i]] + jnp.where(same_blk, gt[i], 0.0) * e_end[rr][:, lsl[i]]

    yield

    y = jnp.concatenate([jnp.concatenate(y_it[rr * N_PAIRS:(rr + 1) * N_PAIRS], axis=1) for rr in range(R)],
                        axis=0)
    inv_n = 1.0 / HEAD_A
    mean = head_sum(y.astype(BF16)) * inv_n
    dlt = y - mean
    var = head_sum((dlt * dlt).astype(BF16)) * inv_n
    yn = dlt * lax.rsqrt(var + GN_EPS_A)
    bonus = head_sum((r * k2 * r_k).astype(BF16)) * v
    out_a = (yn * ln_w + ln_b + bonus) * g

    zb = z[:, P_A:]
    cosf = cos_ref[...]
    sinf = sin_ref[...]
    ti = lax.broadcasted_iota(jnp.int32, (C, C), 0)
    sj = lax.broadcasted_iota(jnp.int32, (C, C), 1)
    relf = (ti - sj).astype(F32)
    causal = sj <= ti
    rowf = lax.broadcasted_iota(jnp.int32, (C, 1), 0).astype(F32)
    hb = range(H_B)
    lgs = [math.log1p(-(2.0 ** (-5.0 - h))) for h in hb]
    decay_in = [jnp.where(causal, jnp.exp(lg * jnp.maximum(relf, 0.0)), 0.0) for lg in lgs]
    q_dec = [jnp.exp(lg * (rowf + 1.0)) for lg in lgs]
    k_dec = [jnp.exp(lg * (float(C - 1) - rowf)) for lg in lgs]
    bitems = [(rr, h) for rr in range(R) for h in hb]
    n_b = range(len(bitems))

    def zcol(i, part):
        rr, h = bitems[i]
        return zb[rws[rr], part * D_B + h * HEAD_B:part * D_B + (h + 1) * HEAD_B]

    def rope(t):
        return t * cosf + pltpu.roll(t, HEAD_B // 2, 1) * sinf

    q_b = [rope(zcol(i, 0)).astype(BF16) for i in n_b]
    k_r = [rope(zcol(i, 1)) * (HEAD_B ** -0.5) for i in n_b]
    v_b = [zcol(i, 2).astype(BF16) for i in n_b]
    r_old = [rb_scr[grp[rr], h] for rr, h in bitems]
    qrk = [_dot_nt(q_b[i], jnp.concatenate([r_old[i].astype(BF16), k_r[i].astype(BF16)], axis=0)) for i in n_b]
    cross = [qrk[i][:, :HEAD_B] * q_dec[bitems[i][1]] for i in n_b]
    scores = [qrk[i][:, HEAD_B:] * decay_in[bitems[i][1]] for i in n_b]
    inner = [_dot(scores[i].astype(BF16), v_b[i]) for i in n_b]
    kv = [_dot_tn(v_b[i], (k_r[i] * k_dec[bitems[i][1]]).astype(BF16)) for i in n_b]
    outs_b = []
    for i in n_b:
        rr, h = bitems[i]
        rb_scr[grp[rr], h] = r_old[i] * math.exp(lgs[h] * C) + kv[i]
        y_h = inner[i] + cross[i]
        mu_h = jnp.mean(y_h, axis=-1, keepdims=True)
        d_h = y_h - mu_h
        var_h = jnp.mean(d_h * d_h, axis=-1, keepdims=True)
        yn_h = d_h * lax.rsqrt(var_h + GN_EPS_B)
        g_h = zcol(i, 3)
        outs_b.append(yn_h * gn_w[:, h * HEAD_B:(h + 1) * HEAD_B] * (g_h * jax.nn.sigmoid(g_h)))
    out_b = jnp.concatenate([jnp.concatenate(outs_b[rr * H_B:(rr + 1) * H_B], axis=1) for rr in range(R)], axis=0)

    mixed = jnp.concatenate([out_a, out_b], axis=1).astype(mixed_ref.dtype)
    for rr in range(R):
        mixed_ref[grp[rr]] = mixed[rws[rr]]

    yield


def _mixer(z3, zprev0, sa0, rb0, cosf, sinf, mu, pvec, wwa_b, wg_b, bd):
    bsz, length, _ = z3.shape
    n_chunks = length // CHUNK
    rows = ROWS_MIX if bsz % ROWS_MIX == 0 else 1
    const2 = lambda b, c: (0, 0)
    const3 = lambda b, c: (0, 0, 0)
    st_shape = (N_PAIRS, LANES, LANES)
    return pl.pallas_call(
        functools.partial(_mixer_kernel, rows=rows),
        grid=(bsz // rows, n_chunks),
        in_specs=[
            pl.BlockSpec((rows, CHUNK, P_IN), lambda b, c: (b, c, 0)),
            pl.BlockSpec((1, P_A), const2),
            pl.BlockSpec(st_shape, const3),
            pl.BlockSpec((H_B, HEAD_B, HEAD_B), const3),
            pl.BlockSpec((CHUNK, HEAD_B), lambda b, c: (c, 0)),
            pl.BlockSpec((CHUNK, HEAD_B), lambda b, c: (c, 0)),
            pl.BlockSpec((1, P_A), const2),
            pl.BlockSpec((8, D_A), const2),
            pl.BlockSpec((LORA_W + LORA_A, 2 * D_A), const2),
            pl.BlockSpec((LORA_G, D_A), const2),
            pl.BlockSpec((HEAD_SUM_W, HEAD_SUM_W), const2),
        ],
        out_specs=[
            pl.BlockSpec((rows, CHUNK, D_A + D_B), lambda b, c: (b, c, 0)),
            pl.BlockSpec((rows,) + st_shape, lambda b, c: (b, 0, 0, 0)),
            pl.BlockSpec((rows, H_B, HEAD_B, HEAD_B), lambda b, c: (b, 0, 0, 0)),
        ],
        out_shape=[
            jax.ShapeDtypeStruct((bsz, length, D_A + D_B), BF16),
            jax.ShapeDtypeStruct((bsz,) + st_shape, F32),
            jax.ShapeDtypeStruct((bsz, H_B, HEAD_B, HEAD_B), F32),
        ],
        scratch_shapes=[
            pltpu.VMEM((rows,) + st_shape, F32),
            pltpu.VMEM((rows, H_B, HEAD_B, HEAD_B), F32),
            pltpu.VMEM((rows, 1, P_A), F32),
        ],
        compiler_params=pltpu.CompilerParams(
            dimension_semantics=("arbitrary", "arbitrary"), vmem_limit_bytes=VMEM_LIMIT),
        name="mixer",
    )(z3, zprev0, sa0, rb0, cosf, sinf, mu, pvec, wwa_b, wg_b, bd)


def _out_router_kernel(mixed_ref, x_ref, wout_ref, g_ref, wrt_hi_ref, wrt_lo_ref, brt_ref, utri_ref,
                       h1_ref, u2_ref, route_t_ref, counts_ref, cnt_scr):
    i = pl.program_id(0)

    @pl.when(i == 0)
    def _():
        cnt_scr[...] = jnp.zeros_like(cnt_scr)

    h1 = x_ref[...] + _dot(mixed_ref[...], wout_ref[...])
    h1_ref[...] = h1
    u2 = _rms(h1, g_ref[...])
    _store_rows(u2_ref, u2)

    u_hi, u_lo = _split2(u2)
    wrt_hi = wrt_hi_ref[...]
    logits = (_dot_nt(wrt_hi, u_hi) + _dot_nt(wrt_hi, u_lo) + _dot_nt(wrt_lo_ref[...], u_hi)
              + brt_ref[...][:, 0:1])

    tm = logits.shape[1]
    rowi = lax.broadcasted_iota(jnp.int32, (ROUTER_ROWS, tm), 0).astype(F32)
    neg = -jnp.inf
    big = float(ROUTER_ROWS)
    first = float(N_GROUPS)

    def rmax(t):
        return jnp.max(t, axis=0, keepdims=True)

    def rsum(t):
        return jnp.sum(t, axis=0, keepdims=True)

    def rmin(t):
        return jnp.min(t, axis=0, keepdims=True)

    gmask = rowi < N_GROUPS
    gmax = rmax(jnp.where(gmask, logits, neg))
    gexp = jnp.where(gmask, jnp.exp(logits - gmax), 0.0)
    gprob = gexp / rsum(gexp)
    g_p = rmax(gprob)
    g_idx = rmin(jnp.where(gmask & (gprob == g_p), rowi, big))

    lo_row = first + EXPERTS_PER_GROUP * g_idx
    emask = (rowi >= lo_row) & (rowi < lo_row + EXPERTS_PER_GROUP)
    emax = rmax(jnp.where(emask, logits, neg))
    eexp = jnp.where(emask, jnp.exp(logits - emax), 0.0)
    eprob = jnp.where(emask, eexp / rsum(eexp), -1.0)
    p1 = rmax(eprob)
    i1 = rmin(jnp.where(eprob == p1, rowi, big))
    eprob2 = jnp.where(rowi == i1, -1.0, eprob)
    p2 = rmax(eprob2)
    i2 = rmin(jnp.where(eprob2 == p2, rowi, big))
    w1 = g_p * p1 / (p1 + p2)
    w2 = g_p * p2 / (p1 + p2)

    sel1 = rowi == i1
    sel2 = rowi == i2
    onehot = jnp.where(sel1 | sel2, 1.0, 0.0)
    before = _dot(onehot.astype(BF16), utri_ref[...]) + cnt_scr[...][:, 0:1]
    rank1 = rsum(jnp.where(sel1, before, 0.0))
    rank2 = rsum(jnp.where(sel2, before, 0.0))
    cnt_scr[...] = cnt_scr[...] + jnp.sum(onehot, axis=1, keepdims=True)
    counts_ref[...] = cnt_scr[...]

    rr = lax.broadcasted_iota(jnp.int32, (ROUTE_ROWS, tm), 0)
    rec = jnp.where(rr == 0, i1 - first, 0.0)
    rec = jnp.where(rr == 1, i2 - first, rec)
    rec = jnp.where(rr == 2, w1, rec)
    rec = jnp.where(rr == 3, w2, rec)
    rec = jnp.where(rr == 4, rank1, rec)
    rec = jnp.where(rr == 5, rank2, rec)
    route_t_ref[...] = rec


def _out_router(mixed2d, x2d, wout_bf16, g, wrt_hi, wrt_lo, brt, utri):
    t, d = x2d.shape
    tm = utri.shape[0]
    c2 = lambda i: (0, 0)
    return pl.pallas_call(
        _out_router_kernel,
        grid=(t // tm,),
        in_specs=[
            pl.BlockSpec((tm, d), lambda i: (i, 0)),
            pl.BlockSpec((tm, d), lambda i: (i, 0)),
            pl.BlockSpec((d, d), c2),
            pl.BlockSpec((1, d), c2),
            pl.BlockSpec((ROUTER_ROWS, d), c2),
            pl.BlockSpec((ROUTER_ROWS, d), c2),
            pl.BlockSpec((ROUTER_ROWS, LANES), c2),
            pl.BlockSpec((tm, tm), c2),
        ],
        out_specs=[
            pl.BlockSpec((tm, d), lambda i: (i, 0)),
            pl.BlockSpec((tm * ROW_TILES, LANES), lambda i: (i, 0)),
            pl.BlockSpec((ROUTE_ROWS, tm), lambda i: (0, i)),
            pl.BlockSpec((ROUTER_ROWS, LANES), c2),
        ],
        out_shape=[
            jax.ShapeDtypeStruct((t, d), F32),
            jax.ShapeDtypeStruct((t * ROW_TILES, LANES), F32),
            jax.ShapeDtypeStruct((ROUTE_ROWS, t), F32),
            jax.ShapeDtypeStruct((ROUTER_ROWS, LANES), F32),
        ],
        scratch_shapes=[pltpu.VMEM((ROUTER_ROWS, LANES), F32)],
        compiler_params=pltpu.CompilerParams(
            dimension_semantics=("arbitrary",), vmem_limit_bytes=VMEM_LIMIT),
        name="out_router",
    )(mixed2d, x2d, wout_bf16, g, wrt_hi, wrt_lo, brt, utri)


def _row_copy(src_ref, src_row, dst_ref, dst_row, sem):
    return pltpu.make_async_copy(_row_tile(src_ref, src_row), _row_tile(dst_ref, dst_row), sem)


def _dispatch_kernel(pad_end_ref, padded_ref, dest_ref, u2_ref, xrows_ref, zero_scr, zsem, sem):
    i = pl.program_id(0)
    tm = u2_ref.shape[0] // ROW_TILES
    blk = BM_MOE * ROW_TILES
    n_blocks = xrows_ref.shape[0] // blk
    n_used = pad_end_ref[N_EXPERTS - 1] // BM_MOE

    def zero_block(start):
        return pltpu.make_async_copy(
            zero_scr, xrows_ref.at[pl.ds(pl.multiple_of(start * ROW_TILES, blk), blk), :], zsem)

    def zero_copy(e):
        return zero_block(pad_end_ref[e] - BM_MOE)

    @pl.when(i == 0)
    def _():
        zero_scr[...] = jnp.zeros_like(zero_scr)
        for e in range(N_EXPERTS):
            @pl.when(padded_ref[e] > 0)
            def _():
                zero_copy(e).start()

        def tail_start(j, carry):
            zero_block(j * BM_MOE).start()
            return carry

        lax.fori_loop(n_used, n_blocks, tail_start, 0)
        for e in range(N_EXPERTS):
            @pl.when(padded_ref[e] > 0)
            def _():
                zero_copy(e).wait()

        def tail_wait(j, carry):
            zero_block(j * BM_MOE).wait()
            return carry

        lax.fori_loop(n_used, n_blocks, tail_wait, 0)

    def issue(jj, carry):
        for u in range(DMA_UNROLL):
            j = jj * DMA_UNROLL + u
            _row_copy(u2_ref, j, xrows_ref, dest_ref[0, 0, j], sem).start(priority=0)
            _row_copy(u2_ref, j, xrows_ref, dest_ref[0, 1, j], sem).start(priority=1)
        return carry

    lax.fori_loop(0, tm // DMA_UNROLL, issue, 0)

    def drain(j, carry):
        _row_copy(u2_ref, j, xrows_ref, dest_ref[0, 0, j], sem).wait()
        _row_copy(u2_ref, j, xrows_ref, dest_ref[0, 1, j], sem).wait()
        return carry

    lax.fori_loop(0, tm, drain, 0, unroll=DMA_UNROLL)


def _dispatch(pad_end, padded, dest3, u2, n_rows):
    t = u2.shape[0] // ROW_TILES
    tm = dest3.shape[2]
    grid_spec = pltpu.PrefetchScalarGridSpec(
        num_scalar_prefetch=2,
        grid=(t // tm,),
        in_specs=[
            pl.BlockSpec((1, 2, tm), lambda i, pe, pd: (i, 0, 0), memory_space=pltpu.SMEM),
            pl.BlockSpec((tm * ROW_TILES, LANES), lambda i, pe, pd: (i, 0)),
        ],
        out_specs=pl.BlockSpec(memory_space=pl.ANY),
        scratch_shapes=[
            pltpu.VMEM((BM_MOE * ROW_TILES, LANES), F32),
            pltpu.SemaphoreType.DMA(()),
            pltpu.SemaphoreType.DMA(()),
        ],
    )
    return pl.pallas_call(
        _dispatch_kernel,
        grid_spec=grid_spec,
        out_shape=jax.ShapeDtypeStruct((n_rows * ROW_TILES, LANES), F32),
        compiler_params=pltpu.CompilerParams(
            dimension_semantics=("arbitrary",), vmem_limit_bytes=VMEM_LIMIT),
        name="dispatch",
    )(pad_end, padded, dest3, u2)


def _expert_ffn_kernel(be_ref, nused_ref, x_ref, wg_ref, wu_ref, wd_ref, y_ref, wg_s, wu_s, wd_s):
    i = pl.program_id(0)
    used = i < nused_ref[0]

    @pl.when(used & ((i == 0) | (be_ref[i] != be_ref[jnp.maximum(i - 1, 0)])))
    def _():
        wg_s[...] = wg_ref[0].astype(BF16)
        wu_s[...] = wu_ref[0].astype(BF16)
        wd_s[...] = wd_ref[0].astype(BF16)

    @pl.when(used)
    def _():
        x = _load_rows(x_ref).astype(BF16)
        hg = _dot(x, wg_s[...])
        hu = _dot(x, wu_s[...])
        hid = (hg * jax.nn.sigmoid(hg)) * hu
        _store_rows(y_ref, _dot(hid.astype(BF16), wd_s[...]))

    @pl.when(jnp.logical_not(used))
    def _():
        y_ref[...] = jnp.zeros_like(y_ref)


def _expert_ffn(block_expert, n_used, x_rows, wg, wu, wd):
    blk = BM_MOE * ROW_TILES
    n_blocks = x_rows.shape[0] // blk
    d, de = wg.shape[1], wg.shape[2]

    def row_map(i, be, nu):
        return (jnp.minimum(i, nu[0] - 1), 0)

    def w_map(i, be, nu):
        return (be[jnp.minimum(i, nu[0] - 1)], 0, 0)

    grid_spec = pltpu.PrefetchScalarGridSpec(
        num_scalar_prefetch=2,
        grid=(n_blocks,),
        in_specs=[
            pl.BlockSpec((blk, LANES), row_map),
            pl.BlockSpec((1, d, de), w_map),
            pl.BlockSpec((1, d, de), w_map),
            pl.BlockSpec((1, de, d), w_map),
        ],
        out_specs=pl.BlockSpec((blk, LANES), lambda i, be, nu: (i, 0)),
        scratch_shapes=[pltpu.VMEM((d, de), BF16), pltpu.VMEM((d, de), BF16), pltpu.VMEM((de, d), BF16)],
    )
    return pl.pallas_call(
        _expert_ffn_kernel,
        grid_spec=grid_spec,
        out_shape=jax.ShapeDtypeStruct(x_rows.shape, F32),
        compiler_params=pltpu.CompilerParams(
            dimension_semantics=("arbitrary",), vmem_limit_bytes=VMEM_LIMIT),
        name="expert_ffn",
    )(block_expert, n_used, x_rows, wg, wu, wd)


def _combine_kernel(dest_ref, dnext_ref, h1_ref, route_ref, g_ref, yrows_ref, out_ref, ybuf, sems):
    i = pl.program_id(0)
    n = pl.num_programs(0)
    tm = h1_ref.shape[0]
    slot = lax.rem(i, 2)

    def copies(dref, s, j):
        return (_row_copy(yrows_ref, dref[0, 0, j], ybuf.at[s, 0], j, sems.at[s]),
                _row_copy(yrows_ref, dref[0, 1, j], ybuf.at[s, 1], j, sems.at[s]))

    def issue_tile(dref, s):
        def body(jj, carry):
            for u in range(DMA_UNROLL):
                c0, c1 = copies(dref, s, jj * DMA_UNROLL + u)
                c0.start(priority=0)
                c1.start(priority=1)
            return carry

        lax.fori_loop(0, tm // DMA_UNROLL, body, 0)

    def drain_tile(dref, s):
        def body(j, carry):
            c0, c1 = copies(dref, s, j)
            c0.wait()
            c1.wait()
            return carry

        lax.fori_loop(0, tm, body, 0, unroll=DMA_UNROLL)

    @pl.when(i == 0)
    def _():
        issue_tile(dest_ref, 0)

    drain_tile(dest_ref, slot)

    for j in range(tm):
        c0, c1 = copies(dnext_ref, 1 - slot, j)
        c0.start(priority=0)
        c1.start(priority=1)

    rec = jnp.concatenate([route_ref[...], jnp.zeros((LANES - ROUTE_ROWS, tm), F32)], axis=0).T
    w1 = rec[:, 2:3]
    w2 = rec[:, 3:4]
    h =h1_ref[...] + _load_rows(ybuf.at[slot, 0]) * w1 + _load_rows(ybuf.at[slot, 1]) * w2
    out_ref[...] = _rms(h, g_ref[...])

    @pl.when(i == n - 1)
    def _():
        drain_tile(dnext_ref, 1 - slot)


def _combine(dest3, h1, route, g, y_rows):
    t, d = h1.shape
    tm = dest3.shape[2]
    n = t // tm
    return pl.pallas_call(
        _combine_kernel,
        grid=(n,),
        in_specs=[
            pl.BlockSpec((1, 2, tm), lambda i: (i, 0, 0), memory_space=pltpu.SMEM),
            pl.BlockSpec((1, 2, tm), lambda i: (jnp.minimum(i + 1, n - 1), 0, 0), memory_space=pltpu.SMEM),
            pl.BlockSpec((tm, d), lambda i: (i, 0)),
            pl.BlockSpec((ROUTE_ROWS, tm), lambda i: (0, i)),
            pl.BlockSpec((1, d), lambda i: (0, 0)),
            pl.BlockSpec(memory_space=pl.ANY),
        ],
        out_specs=pl.BlockSpec((tm, d), lambda i: (i, 0)),
        out_shape=jax.ShapeDtypeStruct((t, d), F32),
        scratch_shapes=[pltpu.VMEM((2, 2, tm * ROW_TILES, LANES), F32), pltpu.SemaphoreType.DMA((2,))],
        compiler_params=pltpu.CompilerParams(
            dimension_semantics=("arbitrary",), vmem_limit_bytes=VMEM_LIMIT),
        name="combine",
    )(dest3, dest3, h1, route, g, y_rows)


def _rope_tables(positions):
    inv_freq = ROPE_BASE ** (-jnp.arange(0, HEAD_B, 2, dtype=F32) / HEAD_B)
    ang = positions.astype(F32)[:, None] * inv_freq[None, :]
    cos, sin = jnp.cos(ang), jnp.sin(ang)
    return jnp.concatenate([cos, cos], axis=1), jnp.concatenate([-sin, sin], axis=1)


def _hi_lo(w):
    hi = w.astype(BF16)
    return hi, (w - hi.astype(F32)).astype(BF16)


def kernel(x, meta_tokens, norm_mix, w_in, shift_mu, decay_w0, decay_up, iclr_a0, iclr_up, gate_up, k_k, k_a, r_k, ln_w_a, ln_b_a, gn_w_b, w_out, norm_ffn, router_group_w, router_group_b, router_expert_w, router_expert_b, moe_w_gate, moe_w_up, moe_w_down, norm_final):
    bsz, seq, d = x.shape
    assert d == D_MODEL and seq % CHUNK == 0 and norm_mix.shape[0] == 1
    t = bsz * seq
    li = 0

    w_in_b = w_in[li].astype(BF16)
    g_mix = norm_mix[li][None, :]
    mu = shift_mu[li][None, :]
    pvec = jnp.stack([decay_w0[li], iclr_a0[li], k_k[li], k_a[li], r_k[li], ln_w_a[li], ln_b_a[li], gn_w_b[li]])
    wwa = jnp.zeros((LORA_W + LORA_A, 2 * D_A), F32)
    wwa = wwa.at[:LORA_W, :D_A].set(decay_up[li]).at[LORA_W:, D_A:].set(iclr_up[li])
    wwa_b = wwa.astype(BF16)
    wg_b = gate_up[li].astype(BF16)
    ch = jnp.arange(HEAD_SUM_W) // HEAD_A
    bd = (ch[:, None] == ch[None, :]).astype(BF16)

    meta_pad = jnp.concatenate([jnp.zeros((CHUNK - N_META, d), F32), meta_tokens.astype(F32)], axis=0)
    z_meta = _in_proj(meta_pad, g_mix, w_in_b)
    cos_m, sin_m = _rope_tables(jnp.arange(CHUNK) - (CHUNK - N_META))
    zeros_a = jnp.zeros((N_PAIRS, LANES, LANES), F32)
    zeros_b = jnp.zeros((H_B, HEAD_B, HEAD_B), F32)
    _, sa_meta, rb_meta = _mixer(z_meta[None], jnp.zeros((1, P_A), F32), zeros_a, zeros_b, cos_m, sin_m,
                                 mu, pvec, wwa_b, wg_b, bd)

    x2d = x.reshape(t, d)
    z = _in_proj(x2d, g_mix, w_in_b)
    cos_x, sin_x = _rope_tables(N_META + jnp.arange(seq))
    mixed, _, _ = _mixer(z.reshape(bsz, seq, P_IN), z_meta[CHUNK - 1:CHUNK, :P_A], sa_meta[0], rb_meta[0],
                         cos_x, sin_x, mu, pvec, wwa_b, wg_b, bd)

    tm_r = min(TM_ROUTE, t)
    n_log = N_GROUPS + N_EXPERTS
    wrt = jnp.zeros((ROUTER_ROWS, d), F32)
    wrt = wrt.at[:N_GROUPS].set(router_group_w[li].T).at[N_GROUPS:n_log].set(router_expert_w[li].T)
    wrt_hi, wrt_lo = _hi_lo(wrt)
    brt = jnp.zeros((ROUTER_ROWS,), F32)
    brt = brt.at[:N_GROUPS].set(router_group_b[li]).at[N_GROUPS:n_log].set(router_expert_b[li])
    brt = jnp.broadcast_to(brt[:, None], (ROUTER_ROWS, LANES))
    ii = jnp.arange(tm_r)
    utri = (ii[:, None] < ii[None, :]).astype(BF16)
    h1, u2, route_t, counts = _out_router(mixed.reshape(t, d), x2d, w_out[li].astype(BF16), norm_ffn[li][None, :],
                                          wrt_hi, wrt_lo, brt, utri)

    n_blocks = -(-(2 * t + N_EXPERTS * (BM_MOE - 1)) // BM_MOE)
    n_rows = n_blocks * BM_MOE
    cnt = counts[N_GROUPS:n_log, 0].astype(jnp.int32)
    padded = (cnt + BM_MOE - 1) // BM_MOE * BM_MOE
    pad_end = jnp.cumsum(padded)
    pad_start = pad_end - padded
    n_used = (pad_end[-1:] // BM_MOE).astype(jnp.int32)
    blk_start = jnp.arange(n_blocks, dtype=jnp.int32) * BM_MOE
    block_expert = jnp.minimum(jnp.sum(pad_end[None, :] <= blk_start[:, None], axis=1), N_EXPERTS - 1).astype(jnp.int32)
    eids = route_t[0:2].astype(jnp.int32)
    ranks = route_t[4:6].astype(jnp.int32)
    sel = eids[:, :, None] == jnp.arange(N_EXPERTS, dtype=jnp.int32)[None, None, :]
    dest = jnp.sum(jnp.where(sel, pad_start[None, None, :], 0), axis=-1) + ranks

    def tiled(tm):
        return dest.reshape(2, t // tm, tm).transpose(1, 0, 2)

    x_rows = _dispatch(pad_end.astype(jnp.int32), padded.astype(jnp.int32), tiled(tm_r), u2, n_rows)
    y_rows = _expert_ffn(block_expert, n_used, x_rows, moe_w_gate[li], moe_w_up[li], moe_w_down[li])
    out = _combine(tiled(min(TM_COMB, t)), h1, route_t, norm_final[None, :], y_rows)
    return out.reshape(bsz, seq, d)
```

```python
import functools
import math

import jax
import jax.numpy as jnp
from jax import lax
from jax.experimental import pallas as pl
from jax.experimental.pallas import tpu as pltpu

F32 = jnp.float32
BF16 = jnp.bfloat16

D_MODEL = 1024
CHUNK = 64
LOG2_CHUNK = 6
N_META = 16
D_A = 512
HEAD_A = 64
LORA_W = 64
LORA_A = 64
LORA_G = 128
DECAY_SCALE = math.exp(-0.5)
GN_EPS_A = 64e-5
D_B = 512
H_B = 4
HEAD_B = 128
ROPE_BASE = 10000.0
GN_EPS_B = 1e-5
P_A = 3 * D_A + LORA_W + LORA_A + LORA_G
P_B = 4 * D_B
P_IN = P_A + P_B
N_GROUPS = 4
EXPERTS_PER_GROUP = 8
N_EXPERTS = N_GROUPS * EXPERTS_PER_GROUP
D_EXPERT = 512
NORM_EPS = 1e-6

LANES = 128
N_PAIRS = D_A // LANES
ROW_TILES = D_MODEL // LANES
HEAD_SUM_W = 256
VMEM_LIMIT = 48 * 1024 * 1024

ROWS_MIX = 8
GROUP_ROWS = 2
MIXER_PHASES = 3
TM_PROJ = 512
TM_ROUTE = 512
ROUTER_PARTS = 2
ROUTER_STAGES = 5
TM_COMB = 512
BM_MOE = 512
ROUTER_ROWS = 40
ROUTE_ROWS = 8
DMA_UNROLL = 8


def _dot(a, b):
    return jnp.dot(a, b, preferred_element_type=F32)


def _dot_nt(a, b):
    return lax.dot_general(a, b, (((1,), (1,)), ((), ())), preferred_element_type=F32)


def _dot_tn(a, b):
    return lax.dot_general(a, b, (((0,), (0,)), ((), ())), preferred_element_type=F32)


def _split2(x):
    hi = x.astype(BF16)
    lo = (x - hi.astype(F32)).astype(BF16)
    return hi, lo


def _rms(x, g):
    return x * lax.rsqrt(jnp.mean(x * x, axis=-1, keepdims=True) + NORM_EPS) * g


def _store_rows(ref, x):
    n = ref.shape[0] // ROW_TILES
    for c in range(ROW_TILES):
        ref[pl.ds(c, n, stride=ROW_TILES), :] = x[:, c * LANES:(c + 1) * LANES]


def _load_rows(ref):
    n = ref.shape[0] // ROW_TILES
    return jnp.concatenate([ref[pl.ds(c, n, stride=ROW_TILES), :] for c in range(ROW_TILES)], axis=1)


def _row_tile(ref, row):
    return ref.at[pl.ds(pl.multiple_of(row * ROW_TILES, ROW_TILES), ROW_TILES), :]


def _in_proj_kernel(x_ref, g_ref, w_ref, z_ref):
    u = _rms(x_ref[...], g_ref[...])
    z_ref[...] = _dot(u.astype(BF16), w_ref[...])


def _in_proj(x2d, g, w_bf16):
    m, d = x2d.shape
    tm = min(TM_PROJ, m)
    n = w_bf16.shape[1]
    return pl.pallas_call(
        _in_proj_kernel,
        grid=(m // tm,),
        in_specs=[
            pl.BlockSpec((tm, d), lambda i: (i, 0)),
            pl.BlockSpec((1, d), lambda i: (0, 0)),
            pl.BlockSpec((d, n), lambda i: (0, 0)),
        ],
        out_specs=pl.BlockSpec((tm, n), lambda i: (i, 0)),
        out_shape=jax.ShapeDtypeStruct((m, n), F32),
        compiler_params=pltpu.CompilerParams(
            dimension_semantics=("arbitrary",), vmem_limit_bytes=VMEM_LIMIT),
        name="in_proj",
    )(x2d, g, w_bf16)


def _mixer_kernel(z_ref, zprev0_ref, sa0_ref, rb0_ref, cos_ref, sin_ref, mu_ref, pvec_ref,
                  wwa_ref, wg_ref, bd_ref,
                  mixed_ref, sa_out_ref, rb_out_ref,
                  sa_scr, rb_scr, zlast_scr, *, rows):
    c_idx = pl.program_id(1)
    n_chunks = pl.num_programs(1)

    @pl.when(c_idx == 0)
    def _():
        for rr in range(rows):
            sa_scr[rr] = sa0_ref[...]
            rb_scr[rr] = rb0_ref[...]
            zlast_scr[rr] = zprev0_ref[...]

    groups = [list(range(g0, min(g0 + GROUP_ROWS, rows))) for g0 in range(0, rows, GROUP_ROWS)]
    phases = [_mixer_group(z_ref, cos_ref, sin_ref, mu_ref, pvec_ref, wwa_ref, wg_ref, bd_ref,
                           mixed_ref, sa_scr, rb_scr, zlast_scr, grp) for grp in groups]
    for _ in range(MIXER_PHASES):
        for ph in phases:
            next(ph)

    @pl.when(c_idx == n_chunks - 1)
    def _():
        sa_out_ref[...] = sa_scr[...]
        rb_out_ref[...] = rb_scr[...]


def _mixer_group(z_ref, cos_ref, sin_ref, mu_ref, pvec_ref, wwa_ref, wg_ref, bd_ref,
                 mixed_ref, sa_scr, rb_scr, zlast_scr, grp):
    C = CHUNK
    R = len(grp)
    M = R * C
    rws = [slice(rr * C, (rr + 1) * C) for rr in range(R)]

    za = jnp.concatenate([z_ref[grp[rr], :, 0:P_A] for rr in range(R)], axis=0)
    row = lax.broadcasted_iota(jnp.int32, (M, 1), 0)
    zprev = pltpu.roll(za, 1, 0)
    for rr in range(R):
        zprev = jnp.where(row == rr * C, zlast_scr[grp[rr]], zprev)
        zlast_scr[grp[rr]] = za[(rr + 1) * C - 1:(rr + 1) * C, :]
    zs = za + (zprev - za) * mu_ref[...]

    r = zs[:, 0:D_A]
    k = zs[:, D_A:2 * D_A]
    v = zs[:, 2 * D_A:3 * D_A]
    wa = zs[:, 3 * D_A:3 * D_A + LORA_W + LORA_A]
    g_lo = zs[:, 3 * D_A + LORA_W + LORA_A:P_A]

    decay_w0 = pvec_ref[0:1, :]
    iclr_a0 = pvec_ref[1:2, :]
    k_k = pvec_ref[2:3, :]
    k_a = pvec_ref[3:4, :]
    r_k = pvec_ref[4:5, :]
    ln_w = pvec_ref[5:6, :]
    ln_b = pvec_ref[6:7, :]
    gn_w = pvec_ref[7:8, :]

    lane = lax.broadcasted_iota(jnp.int32, (1, LANES), 1)
    wa_act = jnp.where(lane < LORA_W, jnp.tanh(wa), wa)
    pre = _dot(wa_act.astype(BF16), wwa_ref[...])
    log_w = -DECAY_SCALE * jax.nn.sigmoid(decay_w0 + pre[:, :D_A])
    a = jax.nn.sigmoid(iclr_a0 + pre[:, D_A:])
    g = _dot(jax.nn.sigmoid(g_lo).astype(BF16), wg_ref[...])

    bd = bd_ref[...]

    def head_sum(t):
        return jnp.concatenate([_dot(t[:, j * HEAD_SUM_W:(j + 1) * HEAD_SUM_W], bd)
                                for j in range(D_A // HEAD_SUM_W)], axis=1)

    kk = k * k_k
    kk = kk / jnp.maximum(jnp.sqrt(head_sum((kk * kk).astype(BF16))), 1e-12)
    k2 = k * (1.0 + (a - 1.0) * k_a)
    b = kk * a

    tm_i = lax.broadcasted_iota(jnp.int32, (M, M), 0)
    sm_j = lax.broadcasted_iota(jnp.int32, (M, M), 1)
    same_row = lax.shift_right_logical(tm_i, LOG2_CHUNK) == lax.shift_right_logical(sm_j, LOG2_CHUNK)
    tril = ((sm_j <= tm_i) & same_row).astype(BF16)
    lw_hi, lw_lo = _split2(log_w)
    cum = _dot(tril, lw_hi) + _dot(tril, lw_lo)
    cmid_r = [cum[rr * C + C // 2 - 1:rr * C + C // 2, :] for rr in range(R)]
    cc = cum - jnp.concatenate([jnp.broadcast_to(cm, (C, D_A)) for cm in cmid_r], axis=0)
    e_nc = jnp.exp(-cc)
    rg = r * jnp.exp(cc)
    kkg = kk * jnp.exp(cc - log_w)
    kinv = k2 * e_nc
    binv = b * e_nc
    e_mid = [jnp.exp(cm) for cm in cmid_r]
    e_end = [jnp.exp(cc[(rr + 1) * C - 1:(rr + 1) * C, :]) for rr in range(R)]
    gam = [jnp.exp(cum[(rr + 1) * C - 1:(rr + 1) * C, :]) for rr in range(R)]

    i2 = lax.broadcasted_iota(jnp.int32, (2 * C, 2 * C), 0)
    j2 = lax.broadcasted_iota(jnp.int32, (2 * C, 2 * C), 1)
    bi = i2 >= C
    bj = j2 >= C
    t2 = jnp.where(bi, i2 - C, i2)
    s2 = jnp.where(bj, j2 - C, j2)
    same_blk = bi == bj
    strict_same = (s2 < t2) & same_blk
    strict_cross = (s2 < t2) & jnp.logical_not(same_blk)
    ta = lax.broadcasted_iota(jnp.int32, (C, 4 * C), 0)
    ja = lax.broadcasted_iota(jnp.int32, (C, 4 * C), 1)
    incl = (ja & (C - 1)) <= ta
    m0 = lane < HEAD_A
    m1 = jnp.logical_not(m0)

    yield

    items =[(rr, p) for rr in range(R) for p in range(N_PAIRS)]
    n_it = range(len(items))
    rsl = [rws[rr] for rr, _ in items]
    lsl = [slice(p * LANES, (p + 1) * LANES) for _, p in items]
    kkg0 = [jnp.where(m0, kkg[rsl[i], lsl[i]], 0.0) for i in n_it]
    kkg1 = [jnp.where(m1, kkg[rsl[i], lsl[i]], 0.0) for i in n_it]
    kinv_i = [kinv[rsl[i], lsl[i]] for i in n_it]
    binv_i = [binv[rsl[i], lsl[i]] for i in n_it]
    v_i = [v[rsl[i], lsl[i]] for i in n_it]
    lhs = [jnp.concatenate([kkg[rsl[i], lsl[i]], rg[rsl[i], lsl[i]]], axis=0).astype(BF16) for i in n_it]
    rhs = [jnp.concatenate([jnp.where(m0, binv_i[i], 0.0), jnp.where(m0, kinv_i[i], 0.0),
                            jnp.where(m1, kinv_i[i], 0.0), jnp.where(m1, binv_i[i], 0.0)], axis=0).astype(BF16)
           for i in n_it]
    out = [_dot_nt(lhs[i], rhs[i]) for i in n_it]
    top = [jnp.concatenate([out[i][:C, :2 * C], out[i][:C, 2 * C:]], axis=0) for i in n_it]
    npow = [jnp.where(strict_same, -top[i], 0.0) for i in n_it]
    q_anti = [jnp.where(strict_cross, top[i], 0.0) for i in n_it]
    vm0 = [jnp.where(m0, v_i[i], 0.0) for i in n_it]
    vm1 = [jnp.where(m1, v_i[i], 0.0) for i in n_it]
    qv = [_dot(q_anti[i].astype(BF16), jnp.concatenate([vm1[i], vm0[i]], axis=0).astype(BF16)) for i in n_it]
    x = [qv[i] + pltpu.roll(jnp.concatenate([kkg0[i], kkg1[i]], axis=0), HEAD_A, 1) for i in n_it]
    for it in range(6):
        nb = [npow[i].astype(BF16) for i in n_it]
        if it < 5:
            prod = [_dot(nb[i], jnp.concatenate([x[i].astype(BF16), nb[i]], axis=1)) for i in n_it]
            x = [x[i] + prod[i][:, :LANES] for i in n_it]
            npow = [prod[i][:, LANES:] for i in n_it]
        else:
            x = [x[i] + _dot(nb[i], x[i].astype(BF16)) for i in n_it]
    w_stack = [jnp.where(same_blk, x[i], 0.0) for i in n_it]
    kkt_stack = [pltpu.roll(jnp.where(same_blk, 0.0, x[i]), HEAD_A, 1) for i in n_it]
    s_old = [sa_scr[grp[rr], p] for rr, p in items]
    s0m = [(s_old[i] * e_mid[items[i][0]][:, lsl[i]]).astype(BF16) for i in n_it]
    u_stack = [_dot_nt(kkt_stack[i].astype(BF16), s0m[i]) + w_stack[i] for i in n_it]
    y_it = [_dot_nt(rg[rsl[i], lsl[i]].astype(BF16), s0m[i]) for i in n_it]
    a_cat = [jnp.where(incl, out[i][C:], 0.0).astype(BF16) for i in n_it]
    y_it = [y_it[i] + _dot(a_cat[i], jnp.concatenate([-u_stack[i][:C], vm0[i], vm1[i], -u_stack[i][C:]],
                                                     axis=0).astype(BF16)) for i in n_it]
    gt = [_dot_tn(jnp.concatenate([v_i[i], -(u_stack[i][:C] + u_stack[i][C:])], axis=0).astype(BF16),
                  jnp.concatenate([kinv_i[i], binv_i[i]], axis=0).astype(BF16)) for i in n_it]
    for i in n_it:
        rr, p = items[i]
        sa_scr[grp[rr], p] = s_old[i] * gam[rr][:, lsl[i]] + jnp.where(same_blk, gt[i], 0.0) * e_end[rr][:, lsl[i]]

    yield

    y = jnp.concatenate([jnp.concatenate(y_it[rr * N_PAIRS:(rr + 1) * N_PAIRS], axis=1) for rr in range(R)],
                        axis=0)
    inv_n = 1.0 / HEAD_A
    mean = head_sum(y.astype(BF16)) * inv_n
    dlt = y - mean
    var = head_sum((dlt * dlt).astype(BF16)) * inv_n
    yn = dlt * lax.rsqrt(var + GN_EPS_A)
    bonus = head_sum((r * k2 * r_k).astype(BF16)) * v
    out_a = (yn * ln_w + ln_b + bonus) * g

    cosf = cos_ref[...]
    sinf = sin_ref[...]
    ti = lax.broadcasted_iota(jnp.int32, (C, C), 0)
    sj = lax.broadcasted_iota(jnp.int32, (C, C), 1)
    relf = (ti - sj).astype(F32)
    causal = sj <= ti
    rowf = lax.broadcasted_iota(jnp.int32, (C, 1), 0).astype(F32)
    hb = range(H_B)
    lgs = [math.log1p(-(2.0 ** (-5.0 - h))) for h in hb]
    decay_in = [jnp.where(causal, jnp.exp(lg * jnp.maximum(relf, 0.0)), 0.0) for lg in lgs]
    q_dec = [jnp.exp(lg * (rowf + 1.0)) for lg in lgs]
    k_dec = [jnp.exp(lg * (float(C - 1) - rowf)) for lg in lgs]
    bitems = [(rr, h) for rr in range(R) for h in hb]
    n_b = range(len(bitems))

    def zcol(i, part):
        rr, h = bitems[i]
        lo = P_A + part * D_B + h * HEAD_B
        return z_ref[grp[rr], :, lo:lo + HEAD_B]

    def rope(t):
        return t * cosf + pltpu.roll(t, HEAD_B // 2, 1) * sinf

    q_b = [rope(zcol(i, 0)).astype(BF16) for i in n_b]
    k_r = [rope(zcol(i, 1)) * (HEAD_B ** -0.5) for i in n_b]
    v_b = [zcol(i, 2).astype(BF16) for i in n_b]
    r_old = [rb_scr[grp[rr], h] for rr, h in bitems]
    qrk = [_dot_nt(q_b[i], jnp.concatenate([r_old[i].astype(BF16), k_r[i].astype(BF16)], axis=0)) for i in n_b]
    cross = [qrk[i][:, :HEAD_B] * q_dec[bitems[i][1]] for i in n_b]
    scores = [qrk[i][:, HEAD_B:] * decay_in[bitems[i][1]] for i in n_b]
    inner = [_dot(scores[i].astype(BF16), v_b[i]) for i in n_b]
    kv = [_dot_tn(v_b[i], (k_r[i] * k_dec[bitems[i][1]]).astype(BF16)) for i in n_b]
    outs_b = []
    for i in n_b:
        rr, h = bitems[i]
        rb_scr[grp[rr], h] = r_old[i] * math.exp(lgs[h] * C) + kv[i]
        y_h = inner[i] + cross[i]
        mu_h = jnp.mean(y_h, axis=-1, keepdims=True)
        d_h = y_h - mu_h
        var_h = jnp.mean(d_h * d_h, axis=-1, keepdims=True)
        yn_h = d_h * lax.rsqrt(var_h + GN_EPS_B)
        g_h = zcol(i, 3)
        outs_b.append(yn_h * gn_w[:, h * HEAD_B:(h + 1) * HEAD_B] * (g_h * jax.nn.sigmoid(g_h)))
    out_b = jnp.concatenate([jnp.concatenate(outs_b[rr * H_B:(rr + 1) * H_B], axis=1) for rr in range(R)], axis=0)

    mixed = jnp.concatenate([out_a, out_b], axis=1).astype(mixed_ref.dtype)
    for rr in range(R):
        mixed_ref[grp[rr]] = mixed[rws[rr]]

    yield


def _mixer(z3, zprev0, sa0, rb0, cosf, sinf, mu, pvec, wwa_b, wg_b, bd):
    bsz, length, _ = z3.shape
    n_chunks = length // CHUNK
    rows = ROWS_MIX if bsz % ROWS_MIX == 0 else 1
    const2 = lambda b, c: (0, 0)
    const3 = lambda b, c: (0, 0, 0)
    st_shape = (N_PAIRS, LANES, LANES)
    return pl.pallas_call(
        functools.partial(_mixer_kernel, rows=rows),
        grid=(bsz // rows, n_chunks),
        in_specs=[
            pl.BlockSpec((rows, CHUNK, P_IN), lambda b, c: (b, c, 0)),
            pl.BlockSpec((1, P_A), const2),
            pl.BlockSpec(st_shape, const3),
            pl.BlockSpec((H_B, HEAD_B, HEAD_B), const3),
            pl.BlockSpec((CHUNK, HEAD_B), lambda b, c: (c, 0)),
            pl.BlockSpec((CHUNK, HEAD_B), lambda b, c: (c, 0)),
            pl.BlockSpec((1, P_A), const2),
            pl.BlockSpec((8, D_A), const2),
            pl.BlockSpec((LORA_W + LORA_A, 2 * D_A), const2),
            pl.BlockSpec((LORA_G, D_A), const2),
            pl.BlockSpec((HEAD_SUM_W, HEAD_SUM_W), const2),
        ],
        out_specs=[
            pl.BlockSpec((rows, CHUNK, D_A + D_B), lambda b, c: (b, c, 0)),
            pl.BlockSpec((rows,) + st_shape, lambda b, c: (b, 0, 0, 0)),
            pl.BlockSpec((rows, H_B, HEAD_B, HEAD_B), lambda b, c: (b, 0, 0, 0)),
        ],
        out_shape=[
            jax.ShapeDtypeStruct((bsz, length, D_A + D_B), BF16),
            jax.ShapeDtypeStruct((bsz,) + st_shape, F32),
            jax.ShapeDtypeStruct((bsz, H_B, HEAD_B, HEAD_B), F32),
        ],
        scratch_shapes=[
            pltpu.VMEM((rows,) + st_shape, F32),
            pltpu.VMEM((rows, H_B, HEAD_B, HEAD_B), F32),
            pltpu.VMEM((rows, 1, P_A), F32),
        ],
        compiler_params=pltpu.CompilerParams(
            dimension_semantics=("arbitrary", "arbitrary"), vmem_limit_bytes=VMEM_LIMIT),
        name="mixer",
    )(z3, zprev0, sa0, rb0, cosf, sinf, mu, pvec, wwa_b, wg_b, bd)


def _out_router_kernel(mixed_ref, x_ref, wout_ref, g_ref, wr_ref, br_ref, utri_ref,
                       h1_ref, u2_ref, route_t_ref, counts_ref, cnt_scr):
    i = pl.program_id(0)

    @pl.when(i == 0)
    def _():
        cnt_scr[...] = jnp.zeros_like(cnt_scr)

    counts = [cnt_scr[...][:, 0:1]]
    parts = [_out_router_part(p, mixed_ref, x_ref, wout_ref, g_ref, wr_ref, br_ref, utri_ref,
                              h1_ref, u2_ref, route_t_ref, counts) for p in range(ROUTER_PARTS)]
    for _ in range(ROUTER_STAGES):
        for part in parts:
            next(part)
    cnt_scr[...] = jnp.broadcast_to(counts[0], cnt_scr.shape)
    counts_ref[...] = cnt_scr[...]


def _out_router_part(p, mixed_ref, x_ref, wout_ref, g_ref, wr_ref, br_ref, utri_ref,
                     h1_ref, u2_ref, route_t_ref, counts):
    tm = utri_ref.shape[0]
    rows = pl.ds(p * tm, tm)
    h1 = x_ref[rows, :] + _dot(mixed_ref[rows, :], wout_ref[...])
    h1_ref[rows, :] = h1
    yield

    u2 = _rms(h1, g_ref[...])
    _store_rows(u2_ref.at[pl.ds(p * tm * ROW_TILES, tm * ROW_TILES), :], u2)

    u_hi, u_lo = _split2(u2)
    hh_hl = _dot(u_hi, wr_ref[...])
    lh = _dot(u_lo, wr_ref[:, 0:LANES])
    logits = (hh_hl[:, :LANES] + hh_hl[:, LANES:] + lh + br_ref[...]).T[:ROUTER_ROWS]
    yield

    rowi = lax.broadcasted_iota(jnp.int32, (ROUTER_ROWS, tm), 0).astype(F32)
    neg = -jnp.inf
    big = float(ROUTER_ROWS)
    first = float(N_GROUPS)

    def rmax(t):
        return jnp.max(t, axis=0, keepdims=True)

    def rsum(t):
        return jnp.sum(t, axis=0, keepdims=True)

    def rmin(t):
        return jnp.min(t, axis=0, keepdims=True)

    gmask = rowi < N_GROUPS
    gmax = rmax(jnp.where(gmask, logits, neg))
    gexp = jnp.where(gmask, jnp.exp(logits - gmax), 0.0)
    gprob = gexp / rsum(gexp)
    g_p = rmax(gprob)
    g_idx = rmin(jnp.where(gmask & (gprob == g_p), rowi, big))
    yield

    lo_row = first + EXPERTS_PER_GROUP * g_idx
    emask = (rowi >= lo_row) & (rowi < lo_row + EXPERTS_PER_GROUP)
    emax = rmax(jnp.where(emask, logits, neg))
    eexp = jnp.where(emask, jnp.exp(logits - emax), 0.0)
    eprob = jnp.where(emask, eexp / rsum(eexp), -1.0)
    p1 = rmax(eprob)
    i1 = rmin(jnp.where(eprob == p1, rowi, big))
    eprob2 = jnp.where(rowi == i1, -1.0, eprob)
    p2 = rmax(eprob2)
    i2 = rmin(jnp.where(eprob2 == p2, rowi, big))
    w1 = g_p * p1 / (p1 + p2)
    w2 = g_p * p2 / (p1 + p2)
    sel1 = rowi == i1
    sel2 = rowi == i2
    onehot = jnp.where(sel1 | sel2, 1.0, 0.0)
    yield

    before = _dot(onehot.astype(BF16), utri_ref[...]) + counts[0]
    rank1 = rsum(jnp.where(sel1, before, 0.0))
    rank2 = rsum(jnp.where(sel2, before, 0.0))
    counts[0] = counts[0] + jnp.sum(onehot, axis=1, keepdims=True)

    rr = lax.broadcasted_iota(jnp.int32, (ROUTE_ROWS, tm), 0)
    rec = jnp.where(rr == 0, i1 - first, 0.0)
    rec = jnp.where(rr == 1, i2 - first, rec)
    rec = jnp.where(rr == 2, w1, rec)
    rec = jnp.where(rr == 3, w2, rec)
    rec = jnp.where(rr == 4, rank1, rec)
    rec = jnp.where(rr == 5, rank2, rec)
    route_t_ref[:, p * tm:(p + 1) * tm] = rec
    yield


def _out_router(mixed2d, x2d, wout_bf16, g, wr_cat, br, utri):
    t, d = x2d.shape
    tp = utri.shape[0]
    tm = tp * ROUTER_PARTS
    c2 = lambda i: (0, 0)
    return pl.pallas_call(
        _out_router_kernel,
        grid=(t // tm,),
        in_specs=[
            pl.BlockSpec((tm, d), lambda i: (i, 0)),
            pl.BlockSpec((tm, d), lambda i: (i, 0)),
            pl.BlockSpec((d, d), c2),
            pl.BlockSpec((1, d), c2),
            pl.BlockSpec((d, 2 * LANES), c2),
            pl.BlockSpec((1, LANES), c2),
            pl.BlockSpec((tp, tp), c2),
        ],
        out_specs=[
            pl.BlockSpec((tm, d), lambda i: (i, 0)),
            pl.BlockSpec((tm * ROW_TILES, LANES), lambda i: (i, 0)),
            pl.BlockSpec((ROUTE_ROWS, tm), lambda i: (0, i)),
            pl.BlockSpec((ROUTER_ROWS, LANES), c2),
        ],
        out_shape=[
            jax.ShapeDtypeStruct((t, d), F32),
            jax.ShapeDtypeStruct((t * ROW_TILES, LANES), F32),
            jax.ShapeDtypeStruct((ROUTE_ROWS, t), F32),
            jax.ShapeDtypeStruct((ROUTER_ROWS, LANES), F32),
        ],
        scratch_shapes=[pltpu.VMEM((ROUTER_ROWS, LANES), F32)],
        compiler_params=pltpu.CompilerParams(
            dimension_semantics=("arbitrary",), vmem_limit_bytes=VMEM_LIMIT),
        name="out_router",
    )(mixed2d, x2d, wout_bf16, g, wr_cat, br, utri)


def _row_copy(src_ref, src_row, dst_ref, dst_row, sem):
    return pltpu.make_async_copy(_row_tile(src_ref, src_row), _row_tile(dst_ref, dst_row), sem)


def _dispatch_kernel(pad_end_ref, padded_ref, dest_ref, u2_ref, xrows_ref, zero_scr, zsem, sem):
    i = pl.program_id(0)
    tm = u2_ref.shape[0] // ROW_TILES
    blk = BM_MOE * ROW_TILES
    n_blocks = xrows_ref.shape[0] // blk
    n_used = pad_end_ref[N_EXPERTS - 1] // BM_MOE

    def zero_block(start):
        return pltpu.make_async_copy(
            zero_scr, xrows_ref.at[pl.ds(pl.multiple_of(start * ROW_TILES, blk), blk), :], zsem)

    def zero_copy(e):
        return zero_block(pad_end_ref[e] - BM_MOE)

    @pl.when(i == 0)
    def _():
        zero_scr[...] = jnp.zeros_like(zero_scr)
        for e in range(N_EXPERTS):
            @pl.when(padded_ref[e] > 0)
            def _():
                zero_copy(e).start()

        def tail_start(j, carry):
            zero_block(j * BM_MOE).start()
            return carry

        lax.fori_loop(n_used, n_blocks, tail_start, 0)
        for e in range(N_EXPERTS):
            @pl.when(padded_ref[e] > 0)
            def _():
                zero_copy(e).wait()

        def tail_wait(j, carry):
            zero_block(j * BM_MOE).wait()
            return carry

        lax.fori_loop(n_used, n_blocks, tail_wait, 0)

    def issue(jj, carry):
        for u in range(DMA_UNROLL):
            j = jj * DMA_UNROLL + u
            _row_copy(u2_ref, j, xrows_ref, dest_ref[0, 0, j], sem).start(priority=0)
            _row_copy(u2_ref, j, xrows_ref, dest_ref[0, 1, j], sem).start(priority=1)
        return carry

    lax.fori_loop(0, tm // DMA_UNROLL, issue, 0)

    def drain(j, carry):
        _row_copy(u2_ref, j, xrows_ref, dest_ref[0, 0, j], sem).wait()
        _row_copy(u2_ref, j, xrows_ref, dest_ref[0, 1, j], sem).wait()
        return carry

    lax.fori_loop(0, tm, drain, 0, unroll=DMA_UNROLL)


def _dispatch(pad_end, padded, dest3, u2, n_rows):
    t = u2.shape[0] // ROW_TILES
    tm = dest3.shape[2]
    grid_spec = pltpu.PrefetchScalarGridSpec(
        num_scalar_prefetch=2,
        grid=(t // tm,),
        in_specs=[
            pl.BlockSpec((1, 2, tm), lambda i, pe, pd: (i, 0, 0), memory_space=pltpu.SMEM),
            pl.BlockSpec((tm * ROW_TILES, LANES), lambda i, pe, pd: (i, 0)),
        ],
        out_specs=pl.BlockSpec(memory_space=pl.ANY),
        scratch_shapes=[
            pltpu.VMEM((BM_MOE * ROW_TILES, LANES), F32),
            pltpu.SemaphoreType.DMA(()),
            pltpu.SemaphoreType.DMA(()),
        ],
    )
    return pl.pallas_call(
        _dispatch_kernel,
        grid_spec=grid_spec,
        out_shape=jax.ShapeDtypeStruct((n_rows * ROW_TILES, LANES), F32),
        compiler_params=pltpu.CompilerParams(
            dimension_semantics=("arbitrary",), vmem_limit_bytes=VMEM_LIMIT),
        name="dispatch",
    )(pad_end, padded, dest3, u2)


def _expert_ffn_kernel(be_ref, nused_ref, x_ref, wg_ref, wu_ref, wd_ref, y_ref, wg_s, wu_s, wd_s):
    i = pl.program_id(0)
    used = i < nused_ref[0]

    @pl.when(used & ((i == 0) | (be_ref[i] != be_ref[jnp.maximum(i - 1, 0)])))
    def _():
        wg_s[...] = wg_ref[0].astype(BF16)
        wu_s[...] = wu_ref[0].astype(BF16)
        wd_s[...] = wd_ref[0].astype(BF16)

    @pl.when(used)
    def _():
        x = _load_rows(x_ref).astype(BF16)
        hg = _dot(x, wg_s[...])
        hu = _dot(x, wu_s[...])
        hid = (hg * jax.nn.sigmoid(hg)) * hu
        _store_rows(y_ref, _dot(hid.astype(BF16), wd_s[...]))

    @pl.when(jnp.logical_not(used))
    def _():
        y_ref[...] = jnp.zeros_like(y_ref)


def _expert_ffn(block_expert, n_used, x_rows, wg, wu, wd):
    blk = BM_MOE * ROW_TILES
    n_blocks = x_rows.shape[0] // blk
    d, de = wg.shape[1], wg.shape[2]

    def row_map(i, be, nu):
        return (jnp.minimum(i, nu[0] - 1), 0)

    def w_map(i, be, nu):
        return (be[jnp.minimum(i, nu[0] - 1)], 0, 0)

    grid_spec = pltpu.PrefetchScalarGridSpec(
        num_scalar_prefetch=2,
        grid=(n_blocks,),
        in_specs=[
            pl.BlockSpec((blk, LANES), row_map),
            pl.BlockSpec((1, d, de), w_map),
            pl.BlockSpec((1, d, de), w_map),
            pl.BlockSpec((1, de, d), w_map),
        ],
        out_specs=pl.BlockSpec((blk, LANES), lambda i, be, nu: (i, 0)),
        scratch_shapes=[pltpu.VMEM((d, de), BF16), pltpu.VMEM((d, de), BF16), pltpu.VMEM((de, d), BF16)],
    )
    return pl.pallas_call(
        _expert_ffn_kernel,
        grid_spec=grid_spec,
        out_shape=jax.ShapeDtypeStruct(x_rows.shape, F32),
        compiler_params=pltpu.CompilerParams(
            dimension_semantics=("arbitrary",), vmem_limit_bytes=VMEM_LIMIT),
        name="expert_ffn",
    )(block_expert, n_used, x_rows, wg, wu, wd)


def _combine_kernel(dest_ref, dnext_ref, h1_ref, route_ref, g_ref, yrows_ref, out_ref, ybuf, sems):
    i = pl.program_id(0)
    n = pl.num_programs(0)
    tm = h1_ref.shape[0]
    slot = lax.rem(i, 2)

    def copies(dref, s, j):
        return (_row_copy(yrows_ref, dref[0, 0, j], ybuf.at[s, 0], j, sems.at[s]),
                _row_copy(yrows_ref, dref[0, 1, j], ybuf.at[s, 1], j, sems.at[s]))

    def issue_tile(dref, s):
        def body(jj, carry):
            for u in range(DMA_UNROLL):
                c0, c1 = copies(dref, s, jj * DMA_UNROLL + u)
                c0.start(priority=0)
                c1.start(priority=1)
            return carry

        lax.fori_loop(0, tm // DMA_UNROLL, body, 0)

    def drain_tile(dref, s):
        def body(j, carry):
            c0, c1 = copies(dref, s, j)
            c0.wait()
            c1.wait()
            return carry

        lax.fori_loop(0, tm, body, 0, unroll=DMA_UNROLL)

    @pl.when(i == 0)
    def _():
        issue_tile(dest_ref, 0)

    drain_tile(dest_ref, slot)

    for j in range(tm):
        c0, c1 = copies(dnext_ref, 1 - slot, j)
        c0.start(priority=0)
        c1.start(priority=1)

    rec = jnp.concatenate([route_ref[...], jnp.zeros((LANES - ROUTE_ROWS, tm), F32)], axis=0).T
    w1 = rec[:, 2:3]
    w2 = rec[:, 3:4]
    h =h1_ref[...] + _load_rows(ybuf.at[slot, 0]) * w1 + _load_rows(ybuf.at[slot, 1]) * w2
    out_ref[...] = _rms(h, g_ref[...])

    @pl.when(i == n - 1)
    def _():
        drain_tile(dnext_ref, 1 - slot)


def _combine(dest3, h1, route, g, y_rows):
    t, d = h1.shape
    tm = dest3.shape[2]
    n = t // tm
    return pl.pallas_call(
        _combine_kernel,
        grid=(n,),
        in_specs=[
            pl.BlockSpec((1, 2, tm), lambda i: (i, 0, 0), memory_space=pltpu.SMEM),
            pl.BlockSpec((1, 2, tm), lambda i: (jnp.minimum(i + 1, n - 1), 0, 0), memory_space=pltpu.SMEM),
            pl.BlockSpec((tm, d), lambda i: (i, 0)),
            pl.BlockSpec((ROUTE_ROWS, tm), lambda i: (0, i)),
            pl.BlockSpec((1, d), lambda i: (0, 0)),
            pl.BlockSpec(memory_space=pl.ANY),
        ],
        out_specs=pl.BlockSpec((tm, d), lambda i: (i, 0)),
        out_shape=jax.ShapeDtypeStruct((t, d), F32),
        scratch_shapes=[pltpu.VMEM((2, 2, tm * ROW_TILES, LANES), F32), pltpu.SemaphoreType.DMA((2,))],
        compiler_params=pltpu.CompilerParams(
            dimension_semantics=("arbitrary",), vmem_limit_bytes=VMEM_LIMIT),
        name="combine",
    )(dest3, dest3, h1, route, g, y_rows)


def _rope_tables(positions):
    inv_freq = ROPE_BASE ** (-jnp.arange(0, HEAD_B, 2, dtype=F32) / HEAD_B)
    ang = positions.astype(F32)[:, None] * inv_freq[None, :]
    cos, sin = jnp.cos(ang), jnp.sin(ang)
    return jnp.concatenate([cos, cos], axis=1), jnp.concatenate([-sin, sin], axis=1)


def _hi_lo(w):
    hi = w.astype(BF16)
    return hi, (w - hi.astype(F32)).astype(BF16)


def kernel(x, meta_tokens, norm_mix, w_in, shift_mu, decay_w0, decay_up, iclr_a0, iclr_up, gate_up, k_k, k_a, r_k, ln_w_a, ln_b_a, gn_w_b, w_out, norm_ffn, router_group_w, router_group_b, router_expert_w, router_expert_b, moe_w_gate, moe_w_up, moe_w_down, norm_final):
    bsz, seq, d = x.shape
    assert d == D_MODEL and seq % CHUNK == 0 and norm_mix.shape[0] == 1
    t = bsz * seq
    li = 0

    w_in_b = w_in[li].astype(BF16)
    g_mix = norm_mix[li][None, :]
    mu = shift_mu[li][None, :]
    pvec = jnp.stack([decay_w0[li], iclr_a0[li], k_k[li], k_a[li], r_k[li], ln_w_a[li], ln_b_a[li], gn_w_b[li]])
    wwa = jnp.zeros((LORA_W + LORA_A, 2 * D_A), F32)
    wwa = wwa.at[:LORA_W, :D_A].set(decay_up[li]).at[LORA_W:, D_A:].set(iclr_up[li])
    wwa_b = wwa.astype(BF16)
    wg_b = gate_up[li].astype(BF16)
    ch = jnp.arange(HEAD_SUM_W) // HEAD_A
    bd = (ch[:, None] == ch[None, :]).astype(BF16)

    meta_pad = jnp.concatenate([jnp.zeros((CHUNK - N_META, d), F32), meta_tokens.astype(F32)], axis=0)
    z_meta = _in_proj(meta_pad, g_mix, w_in_b)
    cos_m, sin_m = _rope_tables(jnp.arange(CHUNK) - (CHUNK - N_META))
    zeros_a = jnp.zeros((N_PAIRS, LANES, LANES), F32)
    zeros_b = jnp.zeros((H_B, HEAD_B, HEAD_B), F32)
    _, sa_meta, rb_meta = _mixer(z_meta[None], jnp.zeros((1, P_A), F32), zeros_a, zeros_b, cos_m, sin_m,
                                 mu, pvec, wwa_b, wg_b, bd)

    x2d = x.reshape(t, d)
    z = _in_proj(x2d, g_mix, w_in_b)
    cos_x, sin_x = _rope_tables(N_META + jnp.arange(seq))
    mixed, _, _ = _mixer(z.reshape(bsz, seq, P_IN), z_meta[CHUNK - 1:CHUNK, :P_A], sa_meta[0], rb_meta[0],
                         cos_x, sin_x, mu, pvec, wwa_b, wg_b, bd)

    tm_r = min(TM_ROUTE, t)
    n_log = N_GROUPS + N_EXPERTS
    wr = jnp.zeros((d, LANES), F32)
    wr = wr.at[:, :N_GROUPS].set(router_group_w[li]).at[:, N_GROUPS:n_log].set(router_expert_w[li])
    wr_cat = jnp.concatenate(_hi_lo(wr), axis=1)
    br = jnp.zeros((1, LANES), F32)
    br = br.at[0, :N_GROUPS].set(router_group_b[li]).at[0, N_GROUPS:n_log].set(router_expert_b[li])
    ii = jnp.arange(tm_r // ROUTER_PARTS)
    utri = (ii[:, None] < ii[None, :]).astype(BF16)
    h1, u2, route_t, counts = _out_router(mixed.reshape(t, d), x2d, w_out[li].astype(BF16), norm_ffn[li][None, :],
                                          wr_cat, br, utri)

    n_blocks = -(-(2 * t + N_EXPERTS * (BM_MOE - 1)) // BM_MOE)
    n_rows = n_blocks * BM_MOE
    cnt = counts[N_GROUPS:n_log, 0].astype(jnp.int32)
    padded = (cnt + BM_MOE - 1) // BM_MOE * BM_MOE
    pad_end = jnp.cumsum(padded)
    pad_start = pad_end - padded
    n_used = (pad_end[-1:] // BM_MOE).astype(jnp.int32)
    blk_start = jnp.arange(n_blocks, dtype=jnp.int32) * BM_MOE
    block_expert = jnp.minimum(jnp.sum(pad_end[None, :] <= blk_start[:, None], axis=1), N_EXPERTS - 1).astype(jnp.int32)
    eids = route_t[0:2].astype(jnp.int32)
    ranks = route_t[4:6].astype(jnp.int32)
    sel = eids[:, :, None] == jnp.arange(N_EXPERTS, dtype=jnp.int32)[None, None, :]
    dest = jnp.sum(jnp.where(sel, pad_start[None, None, :], 0), axis=-1) + ranks

    def tiled(tm):
        return dest.reshape(2, t // tm, tm).transpose(1, 0, 2)

    x_rows = _dispatch(pad_end.astype(jnp.int32), padded.astype(jnp.int32), tiled(tm_r), u2, n_rows)
    y_rows = _expert_ffn(block_expert, n_used, x_rows, moe_w_gate[li], moe_w_up[li], moe_w_down[li])
    out = _combine(tiled(min(TM_COMB, t)), h1, route_t, norm_final[None, :], y_rows)
    return out.reshape(bsz, seq, d)
```

```python
import functools
import math

import jax
import jax.numpy as jnp
from jax import lax
from jax.experimental import pallas as pl
from jax.experimental.pallas import tpu as pltpu

F32 = jnp.float32
BF16 = jnp.bfloat16

D_MODEL = 1024
CHUNK = 64
LOG2_CHUNK = 6
N_META = 16
D_A = 512
HEAD_A = 64
LORA_W = 64
LORA_A = 64
LORA_G = 128
DECAY_SCALE = math.exp(-0.5)
GN_EPS_A = 64e-5
D_B = 512
H_B = 4
HEAD_B = 128
ROPE_BASE = 10000.0
GN_EPS_B = 1e-5
P_A = 3 * D_A + LORA_W + LORA_A + LORA_G
P_B = 4 * D_B
P_IN = P_A + P_B
N_GROUPS = 4
EXPERTS_PER_GROUP = 8
N_EXPERTS = N_GROUPS * EXPERTS_PER_GROUP
D_EXPERT = 512
NORM_EPS = 1e-6

LANES = 128
N_PAIRS = D_A // LANES
ROW_TILES = D_MODEL // LANES
HEAD_SUM_W = 256
VMEM_LIMIT = 48 * 1024 * 1024

ROWS_MIX = 8
GROUP_ROWS = 4
MIXER_PHASES = 3
TM_PROJ = 512
TM_ROUTE = 512
ROUTER_PARTS = 2
ROUTER_STAGES = 5
TM_COMB = 512
BM_MOE = 512
ROUTER_ROWS = 40
ROUTE_ROWS = 8
DMA_UNROLL = 8


def _dot(a, b):
    return jnp.dot(a, b, preferred_element_type=F32)


def _dot_nt(a, b):
    return lax.dot_general(a, b, (((1,), (1,)), ((), ())), preferred_element_type=F32)


def _dot_tn(a, b):
    return lax.dot_general(a, b, (((0,), (0,)), ((), ())), preferred_element_type=F32)


def _split2(x):
    hi = x.astype(BF16)
    lo = (x - hi.astype(F32)).astype(BF16)
    return hi, lo


def _rms(x, g):
    return x * lax.rsqrt(jnp.mean(x * x, axis=-1, keepdims=True) + NORM_EPS) * g


def _store_rows(ref, x):
    n = ref.shape[0] // ROW_TILES
    for c in range(ROW_TILES):
        ref[pl.ds(c, n, stride=ROW_TILES), :] = x[:, c * LANES:(c + 1) * LANES]


def _load_rows(ref):
    n = ref.shape[0] // ROW_TILES
    return jnp.concatenate([ref[pl.ds(c, n, stride=ROW_TILES), :] for c in range(ROW_TILES)], axis=1)


def _row_tile(ref, row):
    return ref.at[pl.ds(pl.multiple_of(row * ROW_TILES, ROW_TILES), ROW_TILES), :]


OPS_A = 7 * D_A
OPS_B = 4 * D_A
CUM_ROWS = 2 * CHUNK


def _in_proj_kernel(x_ref, g_ref, w_ref, zprev0_ref, cos_ref, sin_ref, mu_ref, pvec_ref, wwa_ref, wg_ref, bd_ref,
                    ops_a_ref, ops_b_ref, ops_c_ref, zlast_ref, zlast_scr, *, tiles_per_row):
    i = pl.program_id(0)
    C = CHUNK
    M = x_ref.shape[0]
    n_ch = M // C

    @pl.when(lax.rem(i, tiles_per_row) == 0)
    def _():
        zlast_scr[...] = zprev0_ref[...]

    u = _rms(x_ref[...], g_ref[...])
    z = _dot(u.astype(BF16), w_ref[...])

    za = z[:, :P_A]
    row = lax.broadcasted_iota(jnp.int32, (M, 1), 0)
    zprev = jnp.where(row == 0, zlast_scr[...], pltpu.roll(za, 1, 0))
    zlast_scr[...] = za[M - 1:M, :]
    zlast_ref[...] = za[M - 1:M, :]
    zs = za + (zprev - za) * mu_ref[...]

    r = zs[:, 0:D_A]
    k = zs[:, D_A:2 * D_A]
    v = zs[:, 2 * D_A:3 * D_A]
    wa = zs[:, 3 * D_A:3 * D_A + LORA_W + LORA_A]
    g_lo = zs[:, 3 * D_A + LORA_W + LORA_A:P_A]

    decay_w0 = pvec_ref[0:1, :]
    iclr_a0 = pvec_ref[1:2, :]
    k_k = pvec_ref[2:3, :]
    k_a = pvec_ref[3:4, :]
    r_k = pvec_ref[4:5, :]
    gn_w = pvec_ref[7:8, :]

    lane = lax.broadcasted_iota(jnp.int32, (1, LANES), 1)
    wa_act = jnp.where(lane < LORA_W, jnp.tanh(wa), wa)
    pre = _dot(wa_act.astype(BF16), wwa_ref[...])
    log_w = -DECAY_SCALE * jax.nn.sigmoid(decay_w0 + pre[:, :D_A])
    a = jax.nn.sigmoid(iclr_a0 + pre[:, D_A:])
    g = _dot(jax.nn.sigmoid(g_lo).astype(BF16), wg_ref[...])

    bd = bd_ref[...]

    def head_sum(t):
        return jnp.concatenate([_dot(t[:, j * HEAD_SUM_W:(j + 1) * HEAD_SUM_W], bd)
                                for j in range(D_A // HEAD_SUM_W)], axis=1)

    kk = k * k_k
    kk = kk / jnp.maximum(jnp.sqrt(head_sum((kk * kk).astype(BF16))), 1e-12)
    k2 = k * (1.0 + (a - 1.0) * k_a)
    b = kk * a
    bonus = head_sum((r * k2 * r_k).astype(BF16)) * v

    cr = min(CUM_ROWS, M)
    tm_i = lax.broadcasted_iota(jnp.int32, (cr, cr), 0)
    sm_j = lax.broadcasted_iota(jnp.int32, (cr, cr), 1)
    same_chunk = lax.shift_right_logical(tm_i, LOG2_CHUNK) == lax.shift_right_logical(sm_j, LOG2_CHUNK)
    tril = ((sm_j <= tm_i) & same_chunk).astype(BF16)
    lw_hi, lw_lo = _split2(log_w)
    cum = jnp.concatenate([_dot(tril, lw_hi[j * cr:(j + 1) * cr]) + _dot(tril, lw_lo[j * cr:(j + 1) * cr])
                           for j in range(M // cr)], axis=0)
    cmid = [cum[c * C + C // 2 - 1:c * C + C // 2, :] for c in range(n_ch)]
    cc = cum - jnp.concatenate([jnp.broadcast_to(cm, (C, D_A)) for cm in cmid], axis=0)
    e_nc = jnp.exp(-cc)
    rg = r * jnp.exp(cc)
    kkg = kk * jnp.exp(cc - log_w)
    kinv = k2 * e_nc
    binv = b * e_nc
    ops_c_ref[...] = jnp.zeros_like(ops_c_ref)
    for c in range(n_ch):
        last = (c + 1) * C - 1
        ops_c_ref[c, 0:1, :] = jnp.exp(cmid[c])
        ops_c_ref[c, 1:2, :] = jnp.exp(cc[last:last + 1, :])
        ops_c_ref[c, 2:3, :] = jnp.exp(cum[last:last + 1, :])

    zb = z[:, P_A:]
    cosf = cos_ref[...]
    sinf = sin_ref[...]

    def rope(t):
        return t * cosf + pltpu.roll(t, HEAD_B // 2, 1) * sinf

    q_r = jnp.concatenate([rope(zb[:, h * HEAD_B:(h + 1) * HEAD_B]) for h in range(H_B)], axis=1)
    k_r = jnp.concatenate([rope(zb[:, D_B + h * HEAD_B:D_B + (h + 1) * HEAD_B]) for h in range(H_B)],
                          axis=1) * (HEAD_B ** -0.5)
    v_b = zb[:, 2 * D_B:3 * D_B]
    g_b = zb[:, 3 * D_B:]

    for j, t in enumerate((rg, kkg, kinv, binv, v, q_r, v_b)):
        ops_a_ref[:, j * D_A:(j + 1) * D_A] = t.astype(BF16)
    for j, t in enumerate((g, bonus * g, gn_w * (g_b * jax.nn.sigmoid(g_b)), k_r)):
        ops_b_ref[:, j * D_A:(j + 1) * D_A] = t


def _in_proj(x2d, g, w_bf16, zprev0, cosf, sinf, mu, pvec, wwa_b, wg_b, bd):
    m, d = x2d.shape
    seq = cosf.shape[0]
    tm = min(TM_PROJ, seq)
    tiles_per_row = seq // tm
    n = w_bf16.shape[1]
    n_ch = tm // CHUNK
    c2 = lambda i: (0, 0)
    pos_map = lambda i: (lax.rem(i, tiles_per_row), 0)
    return pl.pallas_call(
        functools.partial(_in_proj_kernel, tiles_per_row=tiles_per_row),
        grid=(m // tm,),
        in_specs=[
            pl.BlockSpec((tm, d), lambda i: (i, 0)),
            pl.BlockSpec((1, d), c2),
            pl.BlockSpec((d, n), c2),
            pl.BlockSpec((1, P_A), c2),
            pl.BlockSpec((tm, HEAD_B), pos_map),
            pl.BlockSpec((tm, HEAD_B), pos_map),
            pl.BlockSpec((1, P_A), c2),
            pl.BlockSpec((8, D_A), c2),
            pl.BlockSpec((LORA_W + LORA_A, 2 * D_A), c2),
            pl.BlockSpec((LORA_G, D_A), c2),
            pl.BlockSpec((HEAD_SUM_W, HEAD_SUM_W), c2),
        ],
        out_specs=[
            pl.BlockSpec((tm, OPS_A), lambda i: (i, 0)),
            pl.BlockSpec((tm, OPS_B), lambda i: (i, 0)),
            pl.BlockSpec((n_ch, 8, D_A), lambda i: (i, 0, 0)),
            pl.BlockSpec((1, P_A), c2),
        ],
        out_shape=[
            jax.ShapeDtypeStruct((m, OPS_A), BF16),
            jax.ShapeDtypeStruct((m, OPS_B), F32),
            jax.ShapeDtypeStruct((m // CHUNK, 8, D_A), F32),
            jax.ShapeDtypeStruct((1, P_A), F32),
        ],
        scratch_shapes=[pltpu.VMEM((1, P_A), F32)],
        compiler_params=pltpu.CompilerParams(
            dimension_semantics=("arbitrary",), vmem_limit_bytes=VMEM_LIMIT),
        name="in_proj",
    )(x2d, g, w_bf16, zprev0, cosf, sinf, mu, pvec, wwa_b, wg_b, bd)


def _mixer_kernel(a_ref, b_ref, c_ref, sa0_ref, rb0_ref, pvec_ref, bd_ref,
                  mixed_ref, sa_out_ref, rb_out_ref,
                  sa_scr, rb_scr, *, rows):
    c_idx = pl.program_id(1)
    n_chunks = pl.num_programs(1)

    @pl.when(c_idx == 0)
    def _():
        for rr in range(rows):
            sa_scr[rr] = sa0_ref[...]
            rb_scr[rr] = rb0_ref[...]

    groups = [list(range(g0, min(g0 + GROUP_ROWS, rows))) for g0 in range(0, rows, GROUP_ROWS)]
    phases = [_mixer_group(a_ref, b_ref, c_ref, pvec_ref, bd_ref, mixed_ref, sa_scr, rb_scr, grp) for grp in groups]
    for _ in range(MIXER_PHASES):
        for ph in phases:
            next(ph)

    @pl.when(c_idx == n_chunks - 1)
    def _():
        sa_out_ref[...] = sa_scr[...]
        rb_out_ref[...] = rb_scr[...]


def _mixer_group(a_ref, b_ref, c_ref, pvec_ref, bd_ref, mixed_ref, sa_scr, rb_scr, grp):
    C = CHUNK
    R = len(grp)
    rws = [slice(rr * C, (rr + 1) * C) for rr in range(R)]

    def ops_a(j):
        return jnp.concatenate([a_ref[grp[rr], :, j * D_A:(j + 1) * D_A] for rr in range(R)], axis=0)

    def ops_b(j):
        return jnp.concatenate([b_ref[grp[rr], :, j * D_A:(j + 1) * D_A] for rr in range(R)], axis=0)

    rg, kkg, kinv, binv, v = (ops_a(j) for j in range(5))
    e_mid = [c_ref[grp[rr], 0, 0:1, :] for rr in range(R)]
    e_end = [c_ref[grp[rr], 0, 1:2, :] for rr in range(R)]
    gam = [c_ref[grp[rr], 0, 2:3, :] for rr in range(R)]
    ln_w = pvec_ref[5:6, :]
    ln_b = pvec_ref[6:7, :]
    lane = lax.broadcasted_iota(jnp.int32, (1, LANES), 1)
    bd = bd_ref[...]

    def head_sum(t):
        return jnp.concatenate([_dot(t[:, j * HEAD_SUM_W:(j + 1) * HEAD_SUM_W], bd)
                                for j in range(D_A // HEAD_SUM_W)], axis=1)

    i2 = lax.broadcasted_iota(jnp.int32, (2 * C, 2 * C), 0)
    j2 = lax.broadcasted_iota(jnp.int32, (2 * C, 2 * C), 1)
    bi = i2 >= C
    bj = j2 >= C
    t2 = jnp.where(bi, i2 - C, i2)
    s2 = jnp.where(bj, j2 - C, j2)
    same_blk = bi == bj
    strict_same = (s2 < t2) & same_blk
    strict_cross = (s2 < t2) & jnp.logical_not(same_blk)
    ta = lax.broadcasted_iota(jnp.int32, (C, 4 * C), 0)
    ja = lax.broadcasted_iota(jnp.int32, (C, 4 * C), 1)
    incl = (ja & (C - 1)) <= ta
    m0 = lane < HEAD_A
    m1 = jnp.logical_not(m0)

    yield

    items =[(rr, p) for rr in range(R) for p in range(N_PAIRS)]
    n_it = range(len(items))
    rsl = [rws[rr] for rr, _ in items]
    lsl = [slice(p * LANES, (p + 1) * LANES) for _, p in items]
    zero_b = jnp.zeros((), BF16)
    kkg_f = [kkg[rsl[i], lsl[i]].astype(F32) for i in n_it]
    kkg0 = [jnp.where(m0, kkg_f[i], 0.0) for i in n_it]
    kkg1 = [jnp.where(m1, kkg_f[i], 0.0) for i in n_it]
    kinv_i = [kinv[rsl[i], lsl[i]] for i in n_it]
    binv_i = [binv[rsl[i], lsl[i]] for i in n_it]
    v_i = [v[rsl[i], lsl[i]] for i in n_it]
    lhs = [jnp.concatenate([kkg[rsl[i], lsl[i]], rg[rsl[i], lsl[i]]], axis=0) for i in n_it]
    rhs = [jnp.concatenate([jnp.where(m0, binv_i[i], zero_b), jnp.where(m0, kinv_i[i], zero_b),
                            jnp.where(m1, kinv_i[i], zero_b), jnp.where(m1, binv_i[i], zero_b)], axis=0)
           for i in n_it]
    out = [_dot_nt(lhs[i], rhs[i]) for i in n_it]
    top = [jnp.concatenate([out[i][:C, :2 * C], out[i][:C, 2 * C:]], axis=0) for i in n_it]
    npow = [jnp.where(strict_same, -top[i], 0.0) for i in n_it]
    q_anti = [jnp.where(strict_cross, top[i], 0.0) for i in n_it]
    vm0 = [jnp.where(m0, v_i[i], zero_b) for i in n_it]
    vm1 = [jnp.where(m1, v_i[i], zero_b) for i in n_it]
    qv = [_dot(q_anti[i].astype(BF16), jnp.concatenate([vm1[i], vm0[i]], axis=0)) for i in n_it]
    x = [qv[i] + pltpu.roll(jnp.concatenate([kkg0[i], kkg1[i]], axis=0), HEAD_A, 1) for i in n_it]
    for it in range(6):
        nb = [npow[i].astype(BF16) for i in n_it]
        if it < 5:
            prod = [_dot(nb[i], jnp.concatenate([x[i].astype(BF16), nb[i]], axis=1)) for i in n_it]
            x = [x[i] + prod[i][:, :LANES] for i in n_it]
            npow = [prod[i][:, LANES:] for i in n_it]
        else:
            x = [x[i] + _dot(nb[i], x[i].astype(BF16)) for i in n_it]
    w_stack = [jnp.where(same_blk, x[i], 0.0) for i in n_it]
    kkt_stack = [pltpu.roll(jnp.where(same_blk, 0.0, x[i]), HEAD_A, 1) for i in n_it]
    s_old = [sa_scr[grp[rr], p] for rr, p in items]
    s0m = [(s_old[i] * e_mid[items[i][0]][:, lsl[i]]).astype(BF16) for i in n_it]
    u_stack = [_dot_nt(kkt_stack[i].astype(BF16), s0m[i]) + w_stack[i] for i in n_it]
    y_it = [_dot_nt(rg[rsl[i], lsl[i]], s0m[i]) for i in n_it]
    a_cat = [jnp.where(incl, out[i][C:], 0.0).astype(BF16) for i in n_it]
    nu = [(-u_stack[i]).astype(BF16) for i in n_it]
    y_it = [y_it[i] + _dot(a_cat[i], jnp.concatenate([nu[i][:C], vm0[i], vm1[i], nu[i][C:]], axis=0))
            for i in n_it]
    gt = [_dot_tn(jnp.concatenate([v_i[i], (-(u_stack[i][:C] + u_stack[i][C:])).astype(BF16)], axis=0),
                  jnp.concatenate([kinv_i[i], binv_i[i]], axis=0)) for i in n_it]
    for i in n_it:
        rr, p = items[i]
        sa_scr[grp[rr], p] = s_old[i] * gam[rr][:, lsl[i]] + jnp.where(same_blk, gt[i], 0.0) * e_end[rr][:, lsl[i]]

    yield

    y = jnp.concatenate([jnp.concatenate(y_it[rr * N_PAIRS:(rr + 1) * N_PAIRS], axis=1) for rr in range(R)],
                        axis=0)
    inv_n = 1.0 / HEAD_A
    mean = head_sum(y.astype(BF16)) * inv_n
    dlt = y - mean
    var = head_sum((dlt * dlt).astype(BF16)) * inv_n
    yn = dlt * lax.rsqrt(var + GN_EPS_A)
    out_a = (yn * ln_w + ln_b) * ops_b(0) + ops_b(1)

    ti = lax.broadcasted_iota(jnp.int32, (C, C), 0)
    sj = lax.broadcasted_iota(jnp.int32, (C, C), 1)
    relf = (ti - sj).astype(F32)
    causal = sj <= ti
    rowf = lax.broadcasted_iota(jnp.int32, (C, 1), 0).astype(F32)
    hb = range(H_B)
    lgs = [math.log1p(-(2.0 ** (-5.0 - h))) for h in hb]
    decay_in = [jnp.where(causal, jnp.exp(lg * jnp.maximum(relf, 0.0)), 0.0) for lg in lgs]
    q_dec = [jnp.exp(lg * (rowf + 1.0)) for lg in lgs]
    k_dec = [jnp.exp(lg * (float(C - 1) - rowf)) for lg in lgs]
    bitems = [(rr, h) for rr in range(R) for h in hb]
    n_b = range(len(bitems))

    def head_cols(ref, j, i):
        rr, h = bitems[i]
        lo = j * D_A + h * HEAD_B
        return ref[grp[rr], :, lo:lo + HEAD_B]

    q_b = [head_cols(a_ref, 5, i) for i in n_b]
    k_r = [head_cols(b_ref, 3, i) for i in n_b]
    v_b = [head_cols(a_ref, 6, i) for i in n_b]
    r_old = [rb_scr[grp[rr], h] for rr, h in bitems]
    qrk = [_dot_nt(q_b[i], jnp.concatenate([r_old[i].astype(BF16), k_r[i].astype(BF16)], axis=0)) for i in n_b]
    cross = [qrk[i][:, :HEAD_B] * q_dec[bitems[i][1]] for i in n_b]
    scores = [qrk[i][:, HEAD_B:] * decay_in[bitems[i][1]] for i in n_b]
    inner = [_dot(scores[i].astype(BF16), v_b[i]) for i in n_b]
    kv = [_dot_tn(v_b[i], (k_r[i] * k_dec[bitems[i][1]]).astype(BF16)) for i in n_b]
    outs_b = []
    for i in n_b:
        rr, h = bitems[i]
        rb_scr[grp[rr], h] = r_old[i] * math.exp(lgs[h] * C) + kv[i]
        y_h = inner[i] + cross[i]
        mu_h = jnp.mean(y_h, axis=-1, keepdims=True)
        d_h = y_h - mu_h
        var_h = jnp.mean(d_h * d_h, axis=-1, keepdims=True)
        yn_h = d_h * lax.rsqrt(var_h + GN_EPS_B)
        outs_b.append(yn_h * head_cols(b_ref, 2, i))
    out_b = jnp.concatenate([jnp.concatenate(outs_b[rr * H_B:(rr + 1) * H_B], axis=1) for rr in range(R)], axis=0)

    mixed = jnp.concatenate([out_a, out_b], axis=1).astype(mixed_ref.dtype)
    for rr in range(R):
        mixed_ref[grp[rr]] = mixed[rws[rr]]

    yield


def _mixer(ops_a, ops_b, ops_c, sa0, rb0, pvec, bd):
    bsz, length, _ = ops_a.shape
    n_chunks = length // CHUNK
    rows = ROWS_MIX if bsz % ROWS_MIX == 0 else 1
    const2 = lambda b, c: (0, 0)
    const3 = lambda b, c: (0, 0, 0)
    st_shape = (N_PAIRS, LANES, LANES)
    return pl.pallas_call(
        functools.partial(_mixer_kernel, rows=rows),
        grid=(bsz // rows, n_chunks),
        in_specs=[
            pl.BlockSpec((rows, CHUNK, OPS_A), lambda b, c: (b, c, 0)),
            pl.BlockSpec((rows, CHUNK, OPS_B), lambda b, c: (b, c, 0)),
            pl.BlockSpec((rows, 1, 8, D_A), lambda b, c: (b, c, 0, 0)),
            pl.BlockSpec(st_shape, const3),
            pl.BlockSpec((H_B, HEAD_B, HEAD_B), const3),
            pl.BlockSpec((8, D_A), const2),
            pl.BlockSpec((HEAD_SUM_W, HEAD_SUM_W), const2),
        ],
        out_specs=[
            pl.BlockSpec((rows, CHUNK, D_A + D_B), lambda b, c: (b, c, 0)),
            pl.BlockSpec((rows,) + st_shape, lambda b, c: (b, 0, 0, 0)),
            pl.BlockSpec((rows, H_B, HEAD_B, HEAD_B), lambda b, c: (b, 0, 0, 0)),
        ],
        out_shape=[
            jax.ShapeDtypeStruct((bsz, length, D_A + D_B), BF16),
            jax.ShapeDtypeStruct((bsz,) + st_shape, F32),
            jax.ShapeDtypeStruct((bsz, H_B, HEAD_B, HEAD_B), F32),
        ],
        scratch_shapes=[
            pltpu.VMEM((rows,) + st_shape, F32),
            pltpu.VMEM((rows, H_B, HEAD_B, HEAD_B), F32),
        ],
        compiler_params=pltpu.CompilerParams(
            dimension_semantics=("arbitrary", "arbitrary"), vmem_limit_bytes=VMEM_LIMIT),
        name="mixer",
    )(ops_a, ops_b, ops_c, sa0, rb0, pvec, bd)


def _out_router_kernel(mixed_ref, x_ref, wout_ref, g_ref, wr_ref, br_ref, utri_ref,
                       h1_ref, u2_ref, route_t_ref, counts_ref, cnt_scr):
    i = pl.program_id(0)

    @pl.when(i == 0)
    def _():
        cnt_scr[...] = jnp.zeros_like(cnt_scr)

    counts = [cnt_scr[...][:, 0:1]]
    parts = [_out_router_part(p, mixed_ref, x_ref, wout_ref, g_ref, wr_ref, br_ref, utri_ref,
                              h1_ref, u2_ref, route_t_ref, counts) for p in range(ROUTER_PARTS)]
    for _ in range(ROUTER_STAGES):
        for part in parts:
            next(part)
    cnt_scr[...] = jnp.broadcast_to(counts[0], cnt_scr.shape)
    counts_ref[...] = cnt_scr[...]


def _out_router_part(p, mixed_ref, x_ref, wout_ref, g_ref, wr_ref, br_ref, utri_ref,
                     h1_ref, u2_ref, route_t_ref, counts):
    tm = utri_ref.shape[0]
    rows = pl.ds(p * tm, tm)
    h1 = x_ref[rows, :] + _dot(mixed_ref[rows, :], wout_ref[...])
    h1_ref[rows, :] = h1
    yield

    u2 = _rms(h1, g_ref[...])
    _store_rows(u2_ref.at[pl.ds(p * tm * ROW_TILES, tm * ROW_TILES), :], u2)

    u_hi, u_lo = _split2(u2)
    hh_hl = _dot(u_hi, wr_ref[...])
    lh = _dot(u_lo, wr_ref[:, 0:LANES])
    logits = (hh_hl[:, :LANES] + hh_hl[:, LANES:] + lh + br_ref[...]).T[:ROUTER_ROWS]
    yield

    rowi = lax.broadcasted_iota(jnp.int32, (ROUTER_ROWS, tm), 0).astype(F32)
    neg = -jnp.inf
    big = float(ROUTER_ROWS)
    first = float(N_GROUPS)

    def rmax(t):
        return jnp.max(t, axis=0, keepdims=True)

    def rsum(t):
        return jnp.sum(t, axis=0, keepdims=True)

    def rmin(t):
        return jnp.min(t, axis=0, keepdims=True)

    gmask = rowi < N_GROUPS
    gmax = rmax(jnp.where(gmask, logits, neg))
    gexp = jnp.where(gmask, jnp.exp(logits - gmax), 0.0)
    gprob = gexp / rsum(gexp)
    g_p = rmax(gprob)
    g_idx = rmin(jnp.where(gmask & (gprob == g_p), rowi, big))
    yield

    lo_row = first + EXPERTS_PER_GROUP * g_idx
    emask = (rowi >= lo_row) & (rowi < lo_row + EXPERTS_PER_GROUP)
    emax = rmax(jnp.where(emask, logits, neg))
    eexp = jnp.where(emask, jnp.exp(logits - emax), 0.0)
    eprob = jnp.where(emask, eexp / rsum(eexp), -1.0)
    p1 = rmax(eprob)
    i1 = rmin(jnp.where(eprob == p1, rowi, big))
    eprob2 = jnp.where(rowi == i1, -1.0, eprob)
    p2 = rmax(eprob2)
    i2 = rmin(jnp.where(eprob2 == p2, rowi, big))
    w1 = g_p * p1 / (p1 + p2)
    w2 = g_p * p2 / (p1 + p2)
    sel1 = rowi == i1
    sel2 = rowi == i2
    onehot = jnp.where(sel1 | sel2, 1.0, 0.0)
    yield

    before = _dot(onehot.astype(BF16), utri_ref[...]) + counts[0]
    rank1 = rsum(jnp.where(sel1, before, 0.0))
    rank2 = rsum(jnp.where(sel2, before, 0.0))
    counts[0] = counts[0] + jnp.sum(onehot, axis=1, keepdims=True)

    rr = lax.broadcasted_iota(jnp.int32, (ROUTE_ROWS, tm), 0)
    rec = jnp.where(rr == 0, i1 - first, 0.0)
    rec = jnp.where(rr == 1, i2 - first, rec)
    rec = jnp.where(rr == 2, w1, rec)
    rec = jnp.where(rr == 3, w2, rec)
    rec = jnp.where(rr == 4, rank1, rec)
    rec = jnp.where(rr == 5, rank2, rec)
    route_t_ref[:, p * tm:(p + 1) * tm] = rec
    yield


def _out_router(mixed2d, x2d, wout_bf16, g, wr_cat, br, utri):
    t, d = x2d.shape
    tp = utri.shape[0]
    tm = tp * ROUTER_PARTS
    c2 = lambda i: (0, 0)
    return pl.pallas_call(
        _out_router_kernel,
        grid=(t // tm,),
        in_specs=[
            pl.BlockSpec((tm, d), lambda i: (i, 0)),
            pl.BlockSpec((tm, d), lambda i: (i, 0)),
            pl.BlockSpec((d, d), c2),
            pl.BlockSpec((1, d), c2),
            pl.BlockSpec((d, 2 * LANES), c2),
            pl.BlockSpec((1, LANES), c2),
            pl.BlockSpec((tp, tp), c2),
        ],
        out_specs=[
            pl.BlockSpec((tm, d), lambda i: (i, 0)),
            pl.BlockSpec((tm * ROW_TILES, LANES), lambda i: (i, 0)),
            pl.BlockSpec((ROUTE_ROWS, tm), lambda i: (0, i)),
            pl.BlockSpec((ROUTER_ROWS, LANES), c2),
        ],
        out_shape=[
            jax.ShapeDtypeStruct((t, d), F32),
            jax.ShapeDtypeStruct((t * ROW_TILES, LANES), F32),
            jax.ShapeDtypeStruct((ROUTE_ROWS, t), F32),
            jax.ShapeDtypeStruct((ROUTER_ROWS, LANES), F32),
        ],
        scratch_shapes=[pltpu.VMEM((ROUTER_ROWS, LANES), F32)],
        compiler_params=pltpu.CompilerParams(
            dimension_semantics=("arbitrary",), vmem_limit_bytes=VMEM_LIMIT),
        name="out_router",
    )(mixed2d, x2d, wout_bf16, g, wr_cat, br, utri)


def _row_copy(src_ref, src_row, dst_ref, dst_row, sem):
    return pltpu.make_async_copy(_row_tile(src_ref, src_row), _row_tile(dst_ref, dst_row), sem)


def _dispatch_kernel(pad_end_ref, padded_ref, dest_ref, u2_ref, xrows_ref, zero_scr, zsem, sem):
    i = pl.program_id(0)
    tm = u2_ref.shape[0] // ROW_TILES
    blk = BM_MOE * ROW_TILES
    n_blocks = xrows_ref.shape[0] // blk
    n_used = pad_end_ref[N_EXPERTS - 1] // BM_MOE

    def zero_block(start):
        return pltpu.make_async_copy(
            zero_scr, xrows_ref.at[pl.ds(pl.multiple_of(start * ROW_TILES, blk), blk), :], zsem)

    def zero_copy(e):
        return zero_block(pad_end_ref[e] - BM_MOE)

    @pl.when(i == 0)
    def _():
        zero_scr[...] = jnp.zeros_like(zero_scr)
        for e in range(N_EXPERTS):
            @pl.when(padded_ref[e] > 0)
            def _():
                zero_copy(e).start()

        def tail_start(j, carry):
            zero_block(j * BM_MOE).start()
            return carry

        lax.fori_loop(n_used, n_blocks, tail_start, 0)
        for e in range(N_EXPERTS):
            @pl.when(padded_ref[e] > 0)
            def _():
                zero_copy(e).wait()

        def tail_wait(j, carry):
            zero_block(j * BM_MOE).wait()
            return carry

        lax.fori_loop(n_used, n_blocks, tail_wait, 0)

    def issue(jj, carry):
        for u in range(DMA_UNROLL):
            j = jj * DMA_UNROLL + u
            _row_copy(u2_ref, j, xrows_ref, dest_ref[0, 0, j], sem).start(priority=0)
            _row_copy(u2_ref, j, xrows_ref, dest_ref[0, 1, j], sem).start(priority=1)
        return carry

    lax.fori_loop(0, tm // DMA_UNROLL, issue, 0)

    def drain(j, carry):
        _row_copy(u2_ref, j, xrows_ref, dest_ref[0, 0, j], sem).wait()
        _row_copy(u2_ref, j, xrows_ref, dest_ref[0, 1, j], sem).wait()
        return carry

    lax.fori_loop(0, tm, drain, 0, unroll=DMA_UNROLL)


def _dispatch(pad_end, padded, dest3, u2, n_rows):
    t = u2.shape[0] // ROW_TILES
    tm = dest3.shape[2]
    grid_spec = pltpu.PrefetchScalarGridSpec(
        num_scalar_prefetch=2,
        grid=(t // tm,),
        in_specs=[
            pl.BlockSpec((1, 2, tm), lambda i, pe, pd: (i, 0, 0), memory_space=pltpu.SMEM),
            pl.BlockSpec((tm * ROW_TILES, LANES), lambda i, pe, pd: (i, 0)),
        ],
        out_specs=pl.BlockSpec(memory_space=pl.ANY),
        scratch_shapes=[
            pltpu.VMEM((BM_MOE * ROW_TILES, LANES), F32),
            pltpu.SemaphoreType.DMA(()),
            pltpu.SemaphoreType.DMA(()),
        ],
    )
    return pl.pallas_call(
        _dispatch_kernel,
        grid_spec=grid_spec,
        out_shape=jax.ShapeDtypeStruct((n_rows * ROW_TILES, LANES), F32),
        compiler_params=pltpu.CompilerParams(
            dimension_semantics=("arbitrary",), vmem_limit_bytes=VMEM_LIMIT),
        name="dispatch",
    )(pad_end, padded, dest3, u2)


def _expert_ffn_kernel(be_ref, nused_ref, x_ref, wg_ref, wu_ref, wd_ref, y_ref, wg_s, wu_s, wd_s):
    i = pl.program_id(0)
    used = i < nused_ref[0]

    @pl.when(used & ((i == 0) | (be_ref[i] != be_ref[jnp.maximum(i - 1, 0)])))
    def _():
        wg_s[...] = wg_ref[0].astype(BF16)
        wu_s[...] = wu_ref[0].astype(BF16)
        wd_s[...] = wd_ref[0].astype(BF16)

    @pl.when(used)
    def _():
        x = _load_rows(x_ref).astype(BF16)
        hg = _dot(x, wg_s[...])
        hu = _dot(x, wu_s[...])
        hid = (hg * jax.nn.sigmoid(hg)) * hu
        _store_rows(y_ref, _dot(hid.astype(BF16), wd_s[...]))

    @pl.when(jnp.logical_not(used))
    def _():
        y_ref[...] = jnp.zeros_like(y_ref)


def _expert_ffn(block_expert, n_used, x_rows, wg, wu, wd):
    blk = BM_MOE * ROW_TILES
    n_blocks = x_rows.shape[0] // blk
    d, de = wg.shape[1], wg.shape[2]

    def row_map(i, be, nu):
        return (jnp.minimum(i, nu[0] - 1), 0)

    def w_map(i, be, nu):
        return (be[jnp.minimum(i, nu[0] - 1)], 0, 0)

    grid_spec = pltpu.PrefetchScalarGridSpec(
        num_scalar_prefetch=2,
        grid=(n_blocks,),
        in_specs=[
            pl.BlockSpec((blk, LANES), row_map),
            pl.BlockSpec((1, d, de), w_map),
            pl.BlockSpec((1, d, de), w_map),
            pl.BlockSpec((1, de, d), w_map),
        ],
        out_specs=pl.BlockSpec((blk, LANES), lambda i, be, nu: (i, 0)),
        scratch_shapes=[pltpu.VMEM((d, de), BF16), pltpu.VMEM((d, de), BF16), pltpu.VMEM((de, d), BF16)],
    )
    return pl.pallas_call(
        _expert_ffn_kernel,
        grid_spec=grid_spec,
        out_shape=jax.ShapeDtypeStruct(x_rows.shape, F32),
        compiler_params=pltpu.CompilerParams(
            dimension_semantics=("arbitrary",), vmem_limit_bytes=VMEM_LIMIT),
        name="expert_ffn",
    )(block_expert, n_used, x_rows, wg, wu, wd)


def _combine_kernel(dest_ref, dnext_ref, h1_ref, route_ref, g_ref, yrows_ref, out_ref, ybuf, sems):
    i = pl.program_id(0)
    n = pl.num_programs(0)
    tm = h1_ref.shape[0]
    slot = lax.rem(i, 2)

    def copies(dref, s, j):
        return (_row_copy(yrows_ref, dref[0, 0, j], ybuf.at[s, 0], j, sems.at[s]),
                _row_copy(yrows_ref, dref[0, 1, j], ybuf.at[s, 1], j, sems.at[s]))

    def issue_tile(dref, s):
        def body(jj, carry):
            for u in range(DMA_UNROLL):
                c0, c1 = copies(dref, s, jj * DMA_UNROLL + u)
                c0.start(priority=0)
                c1.start(priority=1)
            return carry

        lax.fori_loop(0, tm // DMA_UNROLL, body, 0)

    def drain_tile(dref, s):
        def body(j, carry):
            c0, c1 = copies(dref, s, j)
            c0.wait()
            c1.wait()
            return carry

        lax.fori_loop(0, tm, body, 0, unroll=DMA_UNROLL)

    @pl.when(i == 0)
    def _():
        issue_tile(dest_ref, 0)

    drain_tile(dest_ref, slot)

    for j in range(tm):
        c0, c1 = copies(dnext_ref, 1 - slot, j)
        c0.start(priority=0)
        c1.start(priority=1)

    rec = jnp.concatenate([route_ref[...], jnp.zeros((LANES - ROUTE_ROWS, tm), F32)], axis=0).T
    w1 = rec[:, 2:3]
    w2 = rec[:, 3:4]
    h =h1_ref[...] + _load_rows(ybuf.at[slot, 0]) * w1 + _load_rows(ybuf.at[slot, 1]) * w2
    out_ref[...] = _rms(h, g_ref[...])

    @pl.when(i == n - 1)
    def _():
        drain_tile(dnext_ref, 1 - slot)


def _combine(dest3, h1, route, g, y_rows):
    t, d = h1.shape
    tm = dest3.shape[2]
    n = t // tm
    return pl.pallas_call(
        _combine_kernel,
        grid=(n,),
        in_specs=[
            pl.BlockSpec((1, 2, tm), lambda i: (i, 0, 0), memory_space=pltpu.SMEM),
            pl.BlockSpec((1, 2, tm), lambda i: (jnp.minimum(i + 1, n - 1), 0, 0), memory_space=pltpu.SMEM),
            pl.BlockSpec((tm, d), lambda i: (i, 0)),
            pl.BlockSpec((ROUTE_ROWS, tm), lambda i: (0, i)),
            pl.BlockSpec((1, d), lambda i: (0, 0)),
            pl.BlockSpec(memory_space=pl.ANY),
        ],
        out_specs=pl.BlockSpec((tm, d), lambda i: (i, 0)),
        out_shape=jax.ShapeDtypeStruct((t, d), F32),
        scratch_shapes=[pltpu.VMEM((2, 2, tm * ROW_TILES, LANES), F32), pltpu.SemaphoreType.DMA((2,))],
        compiler_params=pltpu.CompilerParams(
            dimension_semantics=("arbitrary",), vmem_limit_bytes=VMEM_LIMIT),
        name="combine",
    )(dest3, dest3, h1, route, g, y_rows)


def _rope_tables(positions):
    inv_freq = ROPE_BASE ** (-jnp.arange(0, HEAD_B, 2, dtype=F32) / HEAD_B)
    ang = positions.astype(F32)[:, None] * inv_freq[None, :]
    cos, sin = jnp.cos(ang), jnp.sin(ang)
    return jnp.concatenate([cos, cos], axis=1), jnp.concatenate([-sin, sin], axis=1)


def _hi_lo(w):
    hi = w.astype(BF16)
    return hi, (w - hi.astype(F32)).astype(BF16)


def kernel(x, meta_tokens, norm_mix, w_in, shift_mu, decay_w0, decay_up, iclr_a0, iclr_up, gate_up, k_k, k_a, r_k, ln_w_a, ln_b_a, gn_w_b, w_out, norm_ffn, router_group_w, router_group_b, router_expert_w, router_expert_b, moe_w_gate, moe_w_up, moe_w_down, norm_final):
    bsz, seq, d = x.shape
    assert d == D_MODEL and seq % CHUNK == 0 and norm_mix.shape[0] == 1
    t = bsz * seq
    li = 0

    w_in_b = w_in[li].astype(BF16)
    g_mix = norm_mix[li][None, :]
    mu = shift_mu[li][None, :]
    pvec = jnp.stack([decay_w0[li], iclr_a0[li], k_k[li], k_a[li], r_k[li], ln_w_a[li], ln_b_a[li], gn_w_b[li]])
    wwa = jnp.zeros((LORA_W + LORA_A, 2 * D_A), F32)
    wwa = wwa.at[:LORA_W, :D_A].set(decay_up[li]).at[LORA_W:, D_A:].set(iclr_up[li])
    wwa_b = wwa.astype(BF16)
    wg_b = gate_up[li].astype(BF16)
    ch = jnp.arange(HEAD_SUM_W) // HEAD_A
    bd = (ch[:, None] == ch[None, :]).astype(BF16)

    meta_pad = jnp.concatenate([jnp.zeros((CHUNK - N_META, d), F32), meta_tokens.astype(F32)], axis=0)
    cos_m, sin_m = _rope_tables(jnp.arange(CHUNK) - (CHUNK - N_META))
    pa_m, pb_m, pc_m, z_last = _in_proj(meta_pad, g_mix, w_in_b, jnp.zeros((1, P_A), F32), cos_m, sin_m,
                                        mu, pvec, wwa_b, wg_b, bd)
    zeros_a = jnp.zeros((N_PAIRS, LANES, LANES), F32)
    zeros_b = jnp.zeros((H_B, HEAD_B, HEAD_B), F32)
    _, sa_meta, rb_meta = _mixer(pa_m[None], pb_m[None], pc_m[None], zeros_a, zeros_b, pvec, bd)

    x2d = x.reshape(t, d)
    cos_x, sin_x = _rope_tables(N_META + jnp.arange(seq))
    pa, pb, pc, _ = _in_proj(x2d, g_mix, w_in_b, z_last, cos_x, sin_x, mu, pvec, wwa_b, wg_b, bd)
    mixed, _, _ = _mixer(pa.reshape(bsz, seq, OPS_A), pb.reshape(bsz, seq, OPS_B),
                         pc.reshape(bsz, seq // CHUNK, 8, D_A), sa_meta[0], rb_meta[0], pvec, bd)

    tm_r = min(TM_ROUTE, t)
    n_log = N_GROUPS + N_EXPERTS
    wr = jnp.zeros((d, LANES), F32)
    wr = wr.at[:, :N_GROUPS].set(router_group_w[li]).at[:, N_GROUPS:n_log].set(router_expert_w[li])
    wr_cat = jnp.concatenate(_hi_lo(wr), axis=1)
    br = jnp.zeros((1, LANES), F32)
    br = br.at[0, :N_GROUPS].set(router_group_b[li]).at[0, N_GROUPS:n_log].set(router_expert_b[li])
    ii = jnp.arange(tm_r // ROUTER_PARTS)
    utri = (ii[:, None] < ii[None, :]).astype(BF16)
    h1, u2, route_t, counts = _out_router(mixed.reshape(t, d), x2d, w_out[li].astype(BF16), norm_ffn[li][None, :],
                                          wr_cat, br, utri)

    n_blocks = -(-(2 * t + N_EXPERTS * (BM_MOE - 1)) // BM_MOE)
    n_rows = n_blocks * BM_MOE
    cnt = counts[N_GROUPS:n_log, 0].astype(jnp.int32)
    padded = (cnt + BM_MOE - 1) // BM_MOE * BM_MOE
    pad_end = jnp.cumsum(padded)
    pad_start = pad_end - padded
    n_used = (pad_end[-1:] // BM_MOE).astype(jnp.int32)
    blk_start = jnp.arange(n_blocks, dtype=jnp.int32) * BM_MOE
    block_expert = jnp.minimum(jnp.sum(pad_end[None, :] <= blk_start[:, None], axis=1), N_EXPERTS - 1).astype(jnp.int32)
    eids = route_t[0:2].astype(jnp.int32)
    ranks = route_t[4:6].astype(jnp.int32)
    sel = eids[:, :, None] == jnp.arange(N_EXPERTS, dtype=jnp.int32)[None, None, :]
    dest = jnp.sum(jnp.where(sel, pad_start[None, None, :], 0), axis=-1) + ranks

    def tiled(tm):
        return dest.reshape(2, t // tm, tm).transpose(1, 0, 2)

    x_rows = _dispatch(pad_end.astype(jnp.int32), padded.astype(jnp.int32), tiled(tm_r), u2, n_rows)
    y_rows = _expert_ffn(block_expert, n_used, x_rows, moe_w_gate[li], moe_w_up[li], moe_w_down[li])
    out = _combine(tiled(min(TM_COMB, t)), h1, route_t, norm_final[None, :], y_rows)
    return out.reshape(bsz, seq, d)
```

```python
import functools
import math

import jax
import jax.numpy as jnp
from jax import lax
from jax.experimental import pallas as pl
from jax.experimental.pallas import tpu as pltpu

F32 = jnp.float32
BF16 = jnp.bfloat16

D_MODEL = 1024
CHUNK = 64
LOG2_CHUNK = 6
N_META = 16
D_A = 512
HEAD_A = 64
LORA_W = 64
LORA_A = 64
LORA_G = 128
DECAY_SCALE = math.exp(-0.5)
GN_EPS_A = 64e-5
D_B = 512
H_B = 4
HEAD_B = 128
ROPE_BASE = 10000.0
GN_EPS_B = 1e-5
P_A = 3 * D_A + LORA_W + LORA_A + LORA_G
P_B = 4 * D_B
P_IN = P_A + P_B
N_GROUPS = 4
EXPERTS_PER_GROUP = 8
N_EXPERTS = N_GROUPS * EXPERTS_PER_GROUP
D_EXPERT = 512
NORM_EPS = 1e-6

LANES = 128
N_PAIRS = D_A // LANES
ROW_TILES = D_MODEL // LANES
HEAD_SUM_W = 256
VMEM_LIMIT = 48 * 1024 * 1024

ROWS_MIX = 8
GROUP_ROWS = 4
MIXER_PHASES = 3
TM_PROJ = 512
TM_ROUTE = 1024
ROUTER_PARTS = 2
ROUTER_STAGES = 5
TM_COMB = 1024
BM_MOE = 512
ROUTER_ROWS = 40
ROUTE_ROWS = 8
DMA_UNROLL = 8


def _dot(a, b):
    return jnp.dot(a, b, preferred_element_type=F32)


def _dot_nt(a, b):
    return lax.dot_general(a, b, (((1,), (1,)), ((), ())), preferred_element_type=F32)


def _dot_tn(a, b):
    return lax.dot_general(a, b, (((0,), (0,)), ((), ())), preferred_element_type=F32)


def _split2(x):
    hi = x.astype(BF16)
    lo = (x - hi.astype(F32)).astype(BF16)
    return hi, lo


def _rms(x, g):
    return x * lax.rsqrt(jnp.mean(x * x, axis=-1, keepdims=True) + NORM_EPS) * g


def _store_rows(ref, x):
    n = ref.shape[0] // ROW_TILES
    for c in range(ROW_TILES):
        ref[pl.ds(c, n, stride=ROW_TILES), :] = x[:, c * LANES:(c + 1) * LANES]


def _load_rows(ref):
    n = ref.shape[0] // ROW_TILES
    return jnp.concatenate([ref[pl.ds(c, n, stride=ROW_TILES), :] for c in range(ROW_TILES)], axis=1)


def _row_tile(ref, row):
    return ref.at[pl.ds(pl.multiple_of(row * ROW_TILES, ROW_TILES), ROW_TILES), :]


OPS_A = 7 * D_A
OPS_B = 4 * D_A
CUM_ROWS = 2 * CHUNK


def _in_proj_kernel(x_ref, g_ref, w_ref, zprev0_ref, cos_ref, sin_ref, mu_ref, pvec_ref, wwa_ref, wg_ref, bd_ref,
                    ops_a_ref, ops_b_ref, ops_c_ref, zlast_ref, zlast_scr, *, tiles_per_row):
    i = pl.program_id(0)
    C = CHUNK
    M = x_ref.shape[0]
    n_ch = M // C

    @pl.when(lax.rem(i, tiles_per_row) == 0)
    def _():
        zlast_scr[...] = zprev0_ref[...]

    u = _rms(x_ref[...], g_ref[...])
    z = _dot(u.astype(BF16), w_ref[...])

    za = z[:, :P_A]
    row = lax.broadcasted_iota(jnp.int32, (M, 1), 0)
    zprev = jnp.where(row == 0, zlast_scr[...], pltpu.roll(za, 1, 0))
    zlast_scr[...] = za[M - 1:M, :]
    zlast_ref[...] = za[M - 1:M, :]
    zs = za + (zprev - za) * mu_ref[...]

    r = zs[:, 0:D_A]
    k = zs[:, D_A:2 * D_A]
    v = zs[:, 2 * D_A:3 * D_A]
    wa = zs[:, 3 * D_A:3 * D_A + LORA_W + LORA_A]
    g_lo = zs[:, 3 * D_A + LORA_W + LORA_A:P_A]

    decay_w0 = pvec_ref[0:1, :]
    iclr_a0 = pvec_ref[1:2, :]
    k_k = pvec_ref[2:3, :]
    k_a = pvec_ref[3:4, :]
    r_k = pvec_ref[4:5, :]
    gn_w = pvec_ref[7:8, :]

    lane = lax.broadcasted_iota(jnp.int32, (1, LANES), 1)
    wa_act = jnp.where(lane < LORA_W, jnp.tanh(wa), wa)
    pre = _dot(wa_act.astype(BF16), wwa_ref[...])
    log_w = -DECAY_SCALE * jax.nn.sigmoid(decay_w0 + pre[:, :D_A])
    a = jax.nn.sigmoid(iclr_a0 + pre[:, D_A:])
    g = _dot(jax.nn.sigmoid(g_lo).astype(BF16), wg_ref[...])

    bd = bd_ref[...]

    def head_sum(t):
        return jnp.concatenate([_dot(t[:, j * HEAD_SUM_W:(j + 1) * HEAD_SUM_W], bd)
                                for j in range(D_A // HEAD_SUM_W)], axis=1)

    kk = k * k_k
    kk = kk / jnp.maximum(jnp.sqrt(head_sum((kk * kk).astype(BF16))), 1e-12)
    k2 = k * (1.0 + (a - 1.0) * k_a)
    b = kk * a
    bonus = head_sum((r * k2 * r_k).astype(BF16)) * v

    cr = min(CUM_ROWS, M)
    tm_i = lax.broadcasted_iota(jnp.int32, (cr, cr), 0)
    sm_j = lax.broadcasted_iota(jnp.int32, (cr, cr), 1)
    same_chunk = lax.shift_right_logical(tm_i, LOG2_CHUNK) == lax.shift_right_logical(sm_j, LOG2_CHUNK)
    tril = ((sm_j <= tm_i) & same_chunk).astype(BF16)
    lw_hi, lw_lo = _split2(log_w)
    cum = jnp.concatenate([_dot(tril, lw_hi[j * cr:(j + 1) * cr]) + _dot(tril, lw_lo[j * cr:(j + 1) * cr])
                           for j in range(M // cr)], axis=0)
    cmid = [cum[c * C + C // 2 - 1:c * C + C // 2, :] for c in range(n_ch)]
    cc = cum - jnp.concatenate([jnp.broadcast_to(cm, (C, D_A)) for cm in cmid], axis=0)
    e_nc = jnp.exp(-cc)
    rg = r * jnp.exp(cc)
    kkg = kk * jnp.exp(cc - log_w)
    kinv = k2 * e_nc
    binv = b * e_nc
    ops_c_ref[...] = jnp.zeros_like(ops_c_ref)
    for c in range(n_ch):
        last = (c + 1) * C - 1
        ops_c_ref[c, 0:1, :] = jnp.exp(cmid[c])
        ops_c_ref[c, 1:2, :] = jnp.exp(cc[last:last + 1, :])
        ops_c_ref[c, 2:3, :] = jnp.exp(cum[last:last + 1, :])

    zb = z[:, P_A:]
    cosf = cos_ref[...]
    sinf = sin_ref[...]

    def rope(t):
        return t * cosf + pltpu.roll(t, HEAD_B // 2, 1) * sinf

    q_r = jnp.concatenate([rope(zb[:, h * HEAD_B:(h + 1) * HEAD_B]) for h in range(H_B)], axis=1)
    k_r = jnp.concatenate([rope(zb[:, D_B + h * HEAD_B:D_B + (h + 1) * HEAD_B]) for h in range(H_B)],
                          axis=1) * (HEAD_B ** -0.5)
    v_b = zb[:, 2 * D_B:3 * D_B]
    g_b = zb[:, 3 * D_B:]

    for j, t in enumerate((rg, kkg, kinv, binv, v, q_r, v_b)):
        ops_a_ref[:, j * D_A:(j + 1) * D_A] = t.astype(BF16)
    for j, t in enumerate((g, bonus * g, gn_w * (g_b * jax.nn.sigmoid(g_b)), k_r)):
        ops_b_ref[:, j * D_A:(j + 1) * D_A] = t


def _in_proj(x2d, g, w_bf16, zprev0, cosf, sinf, mu, pvec, wwa_b, wg_b, bd):
    m, d = x2d.shape
    seq = cosf.shape[0]
    tm = min(TM_PROJ, seq)
    tiles_per_row = seq // tm
    n = w_bf16.shape[1]
    n_ch = tm // CHUNK
    c2 = lambda i: (0, 0)
    pos_map = lambda i: (lax.rem(i, tiles_per_row), 0)
    return pl.pallas_call(
        functools.partial(_in_proj_kernel, tiles_per_row=tiles_per_row),
        grid=(m // tm,),
        in_specs=[
            pl.BlockSpec((tm, d), lambda i: (i, 0)),
            pl.BlockSpec((1, d), c2),
            pl.BlockSpec((d, n), c2),
            pl.BlockSpec((1, P_A), c2),
            pl.BlockSpec((tm, HEAD_B), pos_map),
            pl.BlockSpec((tm, HEAD_B), pos_map),
            pl.BlockSpec((1, P_A), c2),
            pl.BlockSpec((8, D_A), c2),
            pl.BlockSpec((LORA_W + LORA_A, 2 * D_A), c2),
            pl.BlockSpec((LORA_G, D_A), c2),
            pl.BlockSpec((HEAD_SUM_W, HEAD_SUM_W), c2),
        ],
        out_specs=[
            pl.BlockSpec((tm, OPS_A), lambda i: (i, 0)),
            pl.BlockSpec((tm, OPS_B), lambda i: (i, 0)),
            pl.BlockSpec((n_ch, 8, D_A), lambda i: (i, 0, 0)),
            pl.BlockSpec((1, P_A), c2),
        ],
        out_shape=[
            jax.ShapeDtypeStruct((m, OPS_A), BF16),
            jax.ShapeDtypeStruct((m, OPS_B), F32),
            jax.ShapeDtypeStruct((m // CHUNK, 8, D_A), F32),
            jax.ShapeDtypeStruct((1, P_A), F32),
        ],
        scratch_shapes=[pltpu.VMEM((1, P_A), F32)],
        compiler_params=pltpu.CompilerParams(
            dimension_semantics=("arbitrary",), vmem_limit_bytes=VMEM_LIMIT),
        name="in_proj",
    )(x2d, g, w_bf16, zprev0, cosf, sinf, mu, pvec, wwa_b, wg_b, bd)


def _mixer_kernel(a_ref, b_ref, c_ref, sa0_ref, rb0_ref, pvec_ref, bd_ref,
                  mixed_ref, sa_out_ref, rb_out_ref,
                  sa_scr, rb_scr, *, rows):
    c_idx = pl.program_id(1)
    n_chunks = pl.num_programs(1)

    @pl.when(c_idx == 0)
    def _():
        for rr in range(rows):
            sa_scr[rr] = sa0_ref[...]
            rb_scr[rr] = rb0_ref[...]

    groups = [list(range(g0, min(g0 + GROUP_ROWS, rows))) for g0 in range(0, rows, GROUP_ROWS)]
    phases = [_mixer_group(a_ref, b_ref, c_ref, pvec_ref, bd_ref, mixed_ref, sa_scr, rb_scr, grp) for grp in groups]
    for _ in range(MIXER_PHASES):
        for ph in phases:
            next(ph)

    @pl.when(c_idx == n_chunks - 1)
    def _():
        sa_out_ref[...] = sa_scr[...]
        rb_out_ref[...] = rb_scr[...]


def _mixer_group(a_ref, b_ref, c_ref, pvec_ref, bd_ref, mixed_ref, sa_scr, rb_scr, grp):
    C = CHUNK
    R = len(grp)
    rws = [slice(rr * C, (rr + 1) * C) for rr in range(R)]

    def ops_a(j):
        return jnp.concatenate([a_ref[grp[rr], :, j * D_A:(j + 1) * D_A] for rr in range(R)], axis=0)

    def ops_b(j):
        return jnp.concatenate([b_ref[grp[rr], :, j * D_A:(j + 1) * D_A] for rr in range(R)], axis=0)

    rg, kkg, kinv, binv, v = (ops_a(j) for j in range(5))
    e_mid = [c_ref[grp[rr], 0, 0:1, :] for rr in range(R)]
    e_end = [c_ref[grp[rr], 0, 1:2, :] for rr in range(R)]
    gam = [c_ref[grp[rr], 0, 2:3, :] for rr in range(R)]
    ln_w = pvec_ref[5:6, :]
    ln_b = pvec_ref[6:7, :]
    lane = lax.broadcasted_iota(jnp.int32, (1, LANES), 1)
    bd = bd_ref[...]

    def head_sum(t):
        return jnp.concatenate([_dot(t[:, j * HEAD_SUM_W:(j + 1) * HEAD_SUM_W], bd)
                                for j in range(D_A // HEAD_SUM_W)], axis=1)

    i2 = lax.broadcasted_iota(jnp.int32, (2 * C, 2 * C), 0)
    j2 = lax.broadcasted_iota(jnp.int32, (2 * C, 2 * C), 1)
    bi = i2 >= C
    bj = j2 >= C
    t2 = jnp.where(bi, i2 - C, i2)
    s2 = jnp.where(bj, j2 - C, j2)
    same_blk = bi == bj
    strict_same = (s2 < t2) & same_blk
    strict_cross = (s2 < t2) & jnp.logical_not(same_blk)
    ta = lax.broadcasted_iota(jnp.int32, (C, 4 * C), 0)
    ja = lax.broadcasted_iota(jnp.int32, (C, 4 * C), 1)
    incl = (ja & (C - 1)) <= ta
    m0 = lane < HEAD_A
    m1 = jnp.logical_not(m0)

    yield

    items =[(rr, p) for rr in range(R) for p in range(N_PAIRS)]
    n_it = range(len(items))
    rsl = [rws[rr] for rr, _ in items]
    lsl = [slice(p * LANES, (p + 1) * LANES) for _, p in items]
    zero_b = jnp.zeros((), BF16)
    kkg_f = [kkg[rsl[i], lsl[i]].astype(F32) for i in n_it]
    kkg0 = [jnp.where(m0, kkg_f[i], 0.0) for i in n_it]
    kkg1 = [jnp.where(m1, kkg_f[i], 0.0) for i in n_it]
    kinv_i = [kinv[rsl[i], lsl[i]] for i in n_it]
    binv_i = [binv[rsl[i], lsl[i]] for i in n_it]
    v_i = [v[rsl[i], lsl[i]] for i in n_it]
    lhs = [jnp.concatenate([kkg[rsl[i], lsl[i]], rg[rsl[i], lsl[i]]], axis=0) for i in n_it]
    rhs = [jnp.concatenate([jnp.where(m0, binv_i[i], zero_b), jnp.where(m0, kinv_i[i], zero_b),
                            jnp.where(m1, kinv_i[i], zero_b), jnp.where(m1, binv_i[i], zero_b)], axis=0)
           for i in n_it]
    out = [_dot_nt(lhs[i], rhs[i]) for i in n_it]
    top = [jnp.concatenate([out[i][:C, :2 * C], out[i][:C, 2 * C:]], axis=0) for i in n_it]
    npow = [jnp.where(strict_same, -top[i], 0.0) for i in n_it]
    q_anti = [jnp.where(strict_cross, top[i], 0.0) for i in n_it]
    vm0 = [jnp.where(m0, v_i[i], zero_b) for i in n_it]
    vm1 = [jnp.where(m1, v_i[i], zero_b) for i in n_it]
    qv = [_dot(q_anti[i].astype(BF16), jnp.concatenate([vm1[i], vm0[i]], axis=0)) for i in n_it]
    x = [qv[i] + pltpu.roll(jnp.concatenate([kkg0[i], kkg1[i]], axis=0), HEAD_A, 1) for i in n_it]
    for it in range(6):
        nb = [npow[i].astype(BF16) for i in n_it]
        if it < 5:
            prod = [_dot(nb[i], jnp.concatenate([x[i].astype(BF16), nb[i]], axis=1)) for i in n_it]
            x = [x[i] + prod[i][:, :LANES] for i in n_it]
            npow = [prod[i][:, LANES:] for i in n_it]
        else:
            x = [x[i] + _dot(nb[i], x[i].astype(BF16)) for i in n_it]
    w_stack = [jnp.where(same_blk, x[i], 0.0) for i in n_it]
    kkt_stack = [pltpu.roll(jnp.where(same_blk, 0.0, x[i]), HEAD_A, 1) for i in n_it]
    s_old = [sa_scr[grp[rr], p] for rr, p in items]
    s0m = [(s_old[i] * e_mid[items[i][0]][:, lsl[i]]).astype(BF16) for i in n_it]
    u_stack = [_dot_nt(kkt_stack[i].astype(BF16), s0m[i]) + w_stack[i] for i in n_it]
    y_it = [_dot_nt(rg[rsl[i], lsl[i]], s0m[i]) for i in n_it]
    a_cat = [jnp.where(incl, out[i][C:], 0.0).astype(BF16) for i in n_it]
    nu = [(-u_stack[i]).astype(BF16) for i in n_it]
    y_it = [y_it[i] + _dot(a_cat[i], jnp.concatenate([nu[i][:C], vm0[i], vm1[i], nu[i][C:]], axis=0))
            for i in n_it]
    gt = [_dot_tn(jnp.concatenate([v_i[i], (-(u_stack[i][:C] + u_stack[i][C:])).astype(BF16)], axis=0),
                  jnp.concatenate([kinv_i[i], binv_i[i]], axis=0)) for i in n_it]
    for i in n_it:
        rr, p = items[i]
        sa_scr[grp[rr], p] = s_old[i] * gam[rr][:, lsl[i]] + jnp.where(same_blk, gt[i], 0.0) * e_end[rr][:, lsl[i]]

    yield

    y = jnp.concatenate([jnp.concatenate(y_it[rr * N_PAIRS:(rr + 1) * N_PAIRS], axis=1) for rr in range(R)],
                        axis=0)
    inv_n = 1.0 / HEAD_A
    mean = head_sum(y.astype(BF16)) * inv_n
    dlt = y - mean
    var = head_sum((dlt * dlt).astype(BF16)) * inv_n
    yn = dlt * lax.rsqrt(var + GN_EPS_A)
    out_a = (yn * ln_w + ln_b) * ops_b(0) + ops_b(1)

    ti = lax.broadcasted_iota(jnp.int32, (C, C), 0)
    sj = lax.broadcasted_iota(jnp.int32, (C, C), 1)
    relf = (ti - sj).astype(F32)
    causal = sj <= ti
    rowf = lax.broadcasted_iota(jnp.int32, (C, 1), 0).astype(F32)
    hb = range(H_B)
    lgs = [math.log1p(-(2.0 ** (-5.0 - h))) for h in hb]
    decay_in = [jnp.where(causal, jnp.exp(lg * jnp.maximum(relf, 0.0)), 0.0) for lg in lgs]
    q_dec = [jnp.exp(lg * (rowf + 1.0)) for lg in lgs]
    k_dec = [jnp.exp(lg * (float(C - 1) - rowf)) for lg in lgs]
    bitems = [(rr, h) for rr in range(R) for h in hb]
    n_b = range(len(bitems))

    def head_cols(ref, j, i):
        rr, h = bitems[i]
        lo = j * D_A + h * HEAD_B
        return ref[grp[rr], :, lo:lo + HEAD_B]

    q_b = [head_cols(a_ref, 5, i) for i in n_b]
    k_r = [head_cols(b_ref, 3, i) for i in n_b]
    v_b = [head_cols(a_ref, 6, i) for i in n_b]
    r_old = [rb_scr[grp[rr], h] for rr, h in bitems]
    qrk = [_dot_nt(q_b[i], jnp.concatenate([r_old[i].astype(BF16), k_r[i].astype(BF16)], axis=0)) for i in n_b]
    cross = [qrk[i][:, :HEAD_B] * q_dec[bitems[i][1]] for i in n_b]
    scores = [qrk[i][:, HEAD_B:] * decay_in[bitems[i][1]] for i in n_b]
    inner = [_dot(scores[i].astype(BF16), v_b[i]) for i in n_b]
    kv = [_dot_tn(v_b[i], (k_r[i] * k_dec[bitems[i][1]]).astype(BF16)) for i in n_b]
    outs_b = []
    for i in n_b:
        rr, h = bitems[i]
        rb_scr[grp[rr], h] = r_old[i] * math.exp(lgs[h] * C) + kv[i]
        y_h = inner[i] + cross[i]
        mu_h = jnp.mean(y_h, axis=-1, keepdims=True)
        d_h = y_h - mu_h
        var_h = jnp.mean(d_h * d_h, axis=-1, keepdims=True)
        yn_h = d_h * lax.rsqrt(var_h + GN_EPS_B)
        outs_b.append(yn_h * head_cols(b_ref, 2, i))
    out_b = jnp.concatenate([jnp.concatenate(outs_b[rr * H_B:(rr + 1) * H_B], axis=1) for rr in range(R)], axis=0)

    mixed = jnp.concatenate([out_a, out_b], axis=1).astype(mixed_ref.dtype)
    for rr in range(R):
        mixed_ref[grp[rr]] = mixed[rws[rr]]

    yield


def _mixer(ops_a, ops_b, ops_c, sa0, rb0, pvec, bd):
    bsz, length, _ = ops_a.shape
    n_chunks = length // CHUNK
    rows = ROWS_MIX if bsz % ROWS_MIX == 0 else 1
    const2 = lambda b, c: (0, 0)
    const3 = lambda b, c: (0, 0, 0)
    st_shape = (N_PAIRS, LANES, LANES)
    return pl.pallas_call(
        functools.partial(_mixer_kernel, rows=rows),
        grid=(bsz // rows, n_chunks),
        in_specs=[
            pl.BlockSpec((rows, CHUNK, OPS_A), lambda b, c: (b, c, 0)),
            pl.BlockSpec((rows, CHUNK, OPS_B), lambda b, c: (b, c, 0)),
            pl.BlockSpec((rows, 1, 8, D_A), lambda b, c: (b, c, 0, 0)),
            pl.BlockSpec(st_shape, const3),
            pl.BlockSpec((H_B, HEAD_B, HEAD_B), const3),
            pl.BlockSpec((8, D_A), const2),
            pl.BlockSpec((HEAD_SUM_W, HEAD_SUM_W), const2),
        ],
        out_specs=[
            pl.BlockSpec((rows, CHUNK, D_A + D_B), lambda b, c: (b, c, 0)),
            pl.BlockSpec((rows,) + st_shape, lambda b, c: (b, 0, 0, 0)),
            pl.BlockSpec((rows, H_B, HEAD_B, HEAD_B), lambda b, c: (b, 0, 0, 0)),
        ],
        out_shape=[
            jax.ShapeDtypeStruct((bsz, length, D_A + D_B), BF16),
            jax.ShapeDtypeStruct((bsz,) + st_shape, F32),
            jax.ShapeDtypeStruct((bsz, H_B, HEAD_B, HEAD_B), F32),
        ],
        scratch_shapes=[
            pltpu.VMEM((rows,) + st_shape, F32),
            pltpu.VMEM((rows, H_B, HEAD_B, HEAD_B), F32),
        ],
        compiler_params=pltpu.CompilerParams(
            dimension_semantics=("arbitrary", "arbitrary"), vmem_limit_bytes=VMEM_LIMIT),
        name="mixer",
    )(ops_a, ops_b, ops_c, sa0, rb0, pvec, bd)


def _out_router_kernel(mixed_ref, x_ref, wout_ref, g_ref, wr_ref, br_ref, utri_ref,
                       h1_ref, u2_ref, route_t_ref, counts_ref, cnt_scr):
    i = pl.program_id(0)

    @pl.when(i == 0)
    def _():
        cnt_scr[...] = jnp.zeros_like(cnt_scr)

    counts = [cnt_scr[...][:, 0:1]]
    parts = [_out_router_part(p, mixed_ref, x_ref, wout_ref, g_ref, wr_ref, br_ref, utri_ref,
                              h1_ref, u2_ref, route_t_ref, counts) for p in range(ROUTER_PARTS)]
    for _ in range(ROUTER_STAGES):
        for part in parts:
            next(part)
    cnt_scr[...] = jnp.broadcast_to(counts[0], cnt_scr.shape)
    counts_ref[...] = cnt_scr[...]


def _out_router_part(p, mixed_ref, x_ref, wout_ref, g_ref, wr_ref, br_ref, utri_ref,
                     h1_ref, u2_ref, route_t_ref, counts):
    tm = utri_ref.shape[0]
    rows = pl.ds(p * tm, tm)
    h1 = x_ref[rows, :] + _dot(mixed_ref[rows, :], wout_ref[...])
    h1_ref[rows, :] = h1
    yield

    u2 = _rms(h1, g_ref[...])
    _store_rows(u2_ref.at[pl.ds(p * tm * ROW_TILES, tm * ROW_TILES), :], u2)

    u_hi, u_lo = _split2(u2)
    hh_hl = _dot(u_hi, wr_ref[...])
    lh = _dot(u_lo, wr_ref[:, 0:LANES])
    logits = (hh_hl[:, :LANES] + hh_hl[:, LANES:] + lh + br_ref[...]).T[:ROUTER_ROWS]
    yield

    rowi = lax.broadcasted_iota(jnp.int32, (ROUTER_ROWS, tm), 0).astype(F32)
    neg = -jnp.inf
    big = float(ROUTER_ROWS)
    first = float(N_GROUPS)

    def rmax(t):
        return jnp.max(t, axis=0, keepdims=True)

    def rsum(t):
        return jnp.sum(t, axis=0, keepdims=True)

    def rmin(t):
        return jnp.min(t, axis=0, keepdims=True)

    gmask = rowi < N_GROUPS
    gmax = rmax(jnp.where(gmask, logits, neg))
    gexp = jnp.where(gmask, jnp.exp(logits - gmax), 0.0)
    gprob = gexp / rsum(gexp)
    g_p = rmax(gprob)
    g_idx = rmin(jnp.where(gmask & (gprob == g_p), rowi, big))
    yield

    lo_row = first + EXPERTS_PER_GROUP * g_idx
    emask = (rowi >= lo_row) & (rowi < lo_row + EXPERTS_PER_GROUP)
    emax = rmax(jnp.where(emask, logits, neg))
    eexp = jnp.where(emask, jnp.exp(logits - emax), 0.0)
    eprob = jnp.where(emask, eexp / rsum(eexp), -1.0)
    p1 = rmax(eprob)
    i1 = rmin(jnp.where(eprob == p1, rowi, big))
    eprob2 = jnp.where(rowi == i1, -1.0, eprob)
    p2 = rmax(eprob2)
    i2 = rmin(jnp.where(eprob2 == p2, rowi, big))
    w1 = g_p * p1 / (p1 + p2)
    w2 = g_p * p2 / (p1 + p2)
    sel1 = rowi == i1
    sel2 = rowi == i2
    onehot = jnp.where(sel1 | sel2, 1.0, 0.0)
    yield

    before = _dot(onehot.astype(BF16), utri_ref[...]) + counts[0]
    rank1 = rsum(jnp.where(sel1, before, 0.0))
    rank2 = rsum(jnp.where(sel2, before, 0.0))
    counts[0] = counts[0] + jnp.sum(onehot, axis=1, keepdims=True)

    rr = lax.broadcasted_iota(jnp.int32, (ROUTE_ROWS, tm), 0)
    rec = jnp.where(rr == 0, i1 - first, 0.0)
    rec = jnp.where(rr == 1, i2 - first, rec)
    rec = jnp.where(rr == 2, w1, rec)
    rec = jnp.where(rr == 3, w2, rec)
    rec = jnp.where(rr == 4, rank1, rec)
    rec = jnp.where(rr == 5, rank2, rec)
    route_t_ref[:, p * tm:(p + 1) * tm] = rec
    yield


def _out_router(mixed2d, x2d, wout_bf16, g, wr_cat, br, utri):
    t, d = x2d.shape
    tp = utri.shape[0]
    tm = tp * ROUTER_PARTS
    c2 = lambda i: (0, 0)
    return pl.pallas_call(
        _out_router_kernel,
        grid=(t // tm,),
        in_specs=[
            pl.BlockSpec((tm, d), lambda i: (i, 0)),
            pl.BlockSpec((tm, d), lambda i: (i, 0)),
            pl.BlockSpec((d, d), c2),
            pl.BlockSpec((1, d), c2),
            pl.BlockSpec((d, 2 * LANES), c2),
            pl.BlockSpec((1, LANES), c2),
            pl.BlockSpec((tp, tp), c2),
        ],
        out_specs=[
            pl.BlockSpec((tm, d), lambda i: (i, 0)),
            pl.BlockSpec((tm * ROW_TILES, LANES), lambda i: (i, 0)),
            pl.BlockSpec((ROUTE_ROWS, tm), lambda i: (0, i)),
            pl.BlockSpec((ROUTER_ROWS, LANES), c2),
        ],
        out_shape=[
            jax.ShapeDtypeStruct((t, d), F32),
            jax.ShapeDtypeStruct((t * ROW_TILES, LANES), F32),
            jax.ShapeDtypeStruct((ROUTE_ROWS, t), F32),
            jax.ShapeDtypeStruct((ROUTER_ROWS, LANES), F32),
        ],
        scratch_shapes=[pltpu.VMEM((ROUTER_ROWS, LANES), F32)],
        compiler_params=pltpu.CompilerParams(
            dimension_semantics=("arbitrary",), vmem_limit_bytes=VMEM_LIMIT),
        name="out_router",
    )(mixed2d, x2d, wout_bf16, g, wr_cat, br, utri)


def _row_copy(src_ref, src_row, dst_ref, dst_row, sem):
    return pltpu.make_async_copy(_row_tile(src_ref, src_row), _row_tile(dst_ref, dst_row), sem)


def _dispatch_kernel(pad_end_ref, padded_ref, dest_ref, u2_ref, xrows_ref, zero_scr, zsem, sem):
    i = pl.program_id(0)
    tm = u2_ref.shape[0] // ROW_TILES
    blk = BM_MOE * ROW_TILES
    n_blocks = xrows_ref.shape[0] // blk
    n_used = pad_end_ref[N_EXPERTS - 1] // BM_MOE

    def zero_block(start):
        return pltpu.make_async_copy(
            zero_scr, xrows_ref.at[pl.ds(pl.multiple_of(start * ROW_TILES, blk), blk), :], zsem)

    def zero_copy(e):
        return zero_block(pad_end_ref[e] - BM_MOE)

    @pl.when(i == 0)
    def _():
        zero_scr[...] = jnp.zeros_like(zero_scr)
        for e in range(N_EXPERTS):
            @pl.when(padded_ref[e] > 0)
            def _():
                zero_copy(e).start()

        def tail_start(j, carry):
            zero_block(j * BM_MOE).start()
            return carry

        lax.fori_loop(n_used, n_blocks, tail_start, 0)
        for e in range(N_EXPERTS):
            @pl.when(padded_ref[e] > 0)
            def _():
                zero_copy(e).wait()

        def tail_wait(j, carry):
            zero_block(j * BM_MOE).wait()
            return carry

        lax.fori_loop(n_used, n_blocks, tail_wait, 0)

    def issue(jj, carry):
        for u in range(DMA_UNROLL):
            j = jj * DMA_UNROLL + u
            _row_copy(u2_ref, j, xrows_ref, dest_ref[0, 0, j], sem).start(priority=0)
            _row_copy(u2_ref, j, xrows_ref, dest_ref[0, 1, j], sem).start(priority=1)
        return carry

    lax.fori_loop(0, tm // DMA_UNROLL, issue, 0)

    def drain(j, carry):
        _row_copy(u2_ref, j, xrows_ref, dest_ref[0, 0, j], sem).wait()
        _row_copy(u2_ref, j, xrows_ref, dest_ref[0, 1, j], sem).wait()
        return carry

    lax.fori_loop(0, tm, drain, 0, unroll=DMA_UNROLL)


def _dispatch(pad_end, padded, dest3, u2, n_rows):
    t = u2.shape[0] // ROW_TILES
    tm = dest3.shape[2]
    grid_spec = pltpu.PrefetchScalarGridSpec(
        num_scalar_prefetch=2,
        grid=(t // tm,),
        in_specs=[
            pl.BlockSpec((1, 2, tm), lambda i, pe, pd: (i, 0, 0), memory_space=pltpu.SMEM),
            pl.BlockSpec((tm * ROW_TILES, LANES), lambda i, pe, pd: (i, 0)),
        ],
        out_specs=pl.BlockSpec(memory_space=pl.ANY),
        scratch_shapes=[
            pltpu.VMEM((BM_MOE * ROW_TILES, LANES), F32),
            pltpu.SemaphoreType.DMA(()),
            pltpu.SemaphoreType.DMA(()),
        ],
    )
    return pl.pallas_call(
        _dispatch_kernel,
        grid_spec=grid_spec,
        out_shape=jax.ShapeDtypeStruct((n_rows * ROW_TILES, LANES), F32),
        compiler_params=pltpu.CompilerParams(
            dimension_semantics=("arbitrary",), vmem_limit_bytes=VMEM_LIMIT),
        name="dispatch",
    )(pad_end, padded, dest3, u2)


def _expert_ffn_kernel(be_ref, nused_ref, x_ref, wg_ref, wu_ref, wd_ref, y_ref, wg_s, wu_s, wd_s):
    i = pl.program_id(0)
    used = i < nused_ref[0]

    @pl.when(used & ((i == 0) | (be_ref[i] != be_ref[jnp.maximum(i - 1, 0)])))
    def _():
        wg_s[...] = wg_ref[0].astype(BF16)
        wu_s[...] = wu_ref[0].astype(BF16)
        wd_s[...] = wd_ref[0].astype(BF16)

    @pl.when(used)
    def _():
        x = _load_rows(x_ref).astype(BF16)
        hg = _dot(x, wg_s[...])
        hu = _dot(x, wu_s[...])
        hid = (hg * jax.nn.sigmoid(hg)) * hu
        _store_rows(y_ref, _dot(hid.astype(BF16), wd_s[...]))

    @pl.when(jnp.logical_not(used))
    def _():
        y_ref[...] = jnp.zeros_like(y_ref)


def _expert_ffn(block_expert, n_used, x_rows, wg, wu, wd):
    blk = BM_MOE * ROW_TILES
    n_blocks = x_rows.shape[0] // blk
    d, de = wg.shape[1], wg.shape[2]

    def row_map(i, be, nu):
        return (jnp.minimum(i, nu[0] - 1), 0)

    def w_map(i, be, nu):
        return (be[jnp.minimum(i, nu[0] - 1)], 0, 0)

    grid_spec = pltpu.PrefetchScalarGridSpec(
        num_scalar_prefetch=2,
        grid=(n_blocks,),
        in_specs=[
            pl.BlockSpec((blk, LANES), row_map),
            pl.BlockSpec((1, d, de), w_map),
            pl.BlockSpec((1, d, de), w_map),
            pl.BlockSpec((1, de, d), w_map),
        ],
        out_specs=pl.BlockSpec((blk, LANES), lambda i, be, nu: (i, 0)),
        scratch_shapes=[pltpu.VMEM((d, de), BF16), pltpu.VMEM((d, de), BF16), pltpu.VMEM((de, d), BF16)],
    )
    return pl.pallas_call(
        _expert_ffn_kernel,
        grid_spec=grid_spec,
        out_shape=jax.ShapeDtypeStruct(x_rows.shape, F32),
        compiler_params=pltpu.CompilerParams(
            dimension_semantics=("arbitrary",), vmem_limit_bytes=VMEM_LIMIT),
        name="expert_ffn",
    )(block_expert, n_used, x_rows, wg, wu, wd)


def _combine_kernel(dest_ref, dnext_ref, h1_ref, route_ref, g_ref, yrows_ref, out_ref, ybuf, sems):
    i = pl.program_id(0)
    n = pl.num_programs(0)
    tm = h1_ref.shape[0]
    slot = lax.rem(i, 2)

    def copies(dref, s, j):
        return (_row_copy(yrows_ref, dref[0, 0, j], ybuf.at[s, 0], j, sems.at[s]),
                _row_copy(yrows_ref, dref[0, 1, j], ybuf.at[s, 1], j, sems.at[s]))

    def issue_tile(dref, s):
        def body(jj, carry):
            for u in range(DMA_UNROLL):
                c0, c1 = copies(dref, s, jj * DMA_UNROLL + u)
                c0.start(priority=0)
                c1.start(priority=1)
            return carry

        lax.fori_loop(0, tm // DMA_UNROLL, body, 0)

    def drain_tile(dref, s):
        def body(j, carry):
            c0, c1 = copies(dref, s, j)
            c0.wait()
            c1.wait()
            return carry

        lax.fori_loop(0, tm, body, 0, unroll=DMA_UNROLL)

    @pl.when(i == 0)
    def _():
        issue_tile(dest_ref, 0)

    drain_tile(dest_ref, slot)

    for j in range(tm):
        c0, c1 = copies(dnext_ref, 1 - slot, j)
        c0.start(priority=0)
        c1.start(priority=1)

    rec = jnp.concatenate([route_ref[...], jnp.zeros((LANES - ROUTE_ROWS, tm), F32)], axis=0).T
    w1 = rec[:, 2:3]
    w2 = rec[:, 3:4]
    h =h1_ref[...] + _load_rows(ybuf.at[slot, 0]) * w1 + _load_rows(ybuf.at[slot, 1]) * w2
    out_ref[...] = _rms(h, g_ref[...])

    @pl.when(i == n - 1)
    def _():
        drain_tile(dnext_ref, 1 - slot)


def _combine(dest3, h1, route, g, y_rows):
    t, d = h1.shape
    tm = dest3.shape[2]
    n = t // tm
    return pl.pallas_call(
        _combine_kernel,
        grid=(n,),
        in_specs=[
            pl.BlockSpec((1, 2, tm), lambda i: (i, 0, 0), memory_space=pltpu.SMEM),
            pl.BlockSpec((1, 2, tm), lambda i: (jnp.minimum(i + 1, n - 1), 0, 0), memory_space=pltpu.SMEM),
            pl.BlockSpec((tm, d), lambda i: (i, 0)),
            pl.BlockSpec((ROUTE_ROWS, tm), lambda i: (0, i)),
            pl.BlockSpec((1, d), lambda i: (0, 0)),
            pl.BlockSpec(memory_space=pl.ANY),
        ],
        out_specs=pl.BlockSpec((tm, d), lambda i: (i, 0)),
        out_shape=jax.ShapeDtypeStruct((t, d), F32),
        scratch_shapes=[pltpu.VMEM((2, 2, tm * ROW_TILES, LANES), F32), pltpu.SemaphoreType.DMA((2,))],
        compiler_params=pltpu.CompilerParams(
            dimension_semantics=("arbitrary",), vmem_limit_bytes=VMEM_LIMIT),
        name="combine",
    )(dest3, dest3, h1, route, g, y_rows)


def _rope_tables(positions):
    inv_freq = ROPE_BASE ** (-jnp.arange(0, HEAD_B, 2, dtype=F32) / HEAD_B)
    ang = positions.astype(F32)[:, None] * inv_freq[None, :]
    cos, sin = jnp.cos(ang), jnp.sin(ang)
    return jnp.concatenate([cos, cos], axis=1), jnp.concatenate([-sin, sin], axis=1)


def _hi_lo(w):
    hi = w.astype(BF16)
    return hi, (w - hi.astype(F32)).astype(BF16)


def kernel(x, meta_tokens, norm_mix, w_in, shift_mu, decay_w0, decay_up, iclr_a0, iclr_up, gate_up, k_k, k_a, r_k, ln_w_a, ln_b_a, gn_w_b, w_out, norm_ffn, router_group_w, router_group_b, router_expert_w, router_expert_b, moe_w_gate, moe_w_up, moe_w_down, norm_final):
    bsz, seq, d = x.shape
    assert d == D_MODEL and seq % CHUNK == 0 and norm_mix.shape[0] == 1
    t = bsz * seq
    li = 0

    w_in_b = w_in[li].astype(BF16)
    g_mix = norm_mix[li][None, :]
    mu = shift_mu[li][None, :]
    pvec = jnp.stack([decay_w0[li], iclr_a0[li], k_k[li], k_a[li], r_k[li], ln_w_a[li], ln_b_a[li], gn_w_b[li]])
    wwa = jnp.zeros((LORA_W + LORA_A, 2 * D_A), F32)
    wwa = wwa.at[:LORA_W, :D_A].set(decay_up[li]).at[LORA_W:, D_A:].set(iclr_up[li])
    wwa_b = wwa.astype(BF16)
    wg_b = gate_up[li].astype(BF16)
    ch = jnp.arange(HEAD_SUM_W) // HEAD_A
    bd = (ch[:, None] == ch[None, :]).astype(BF16)

    meta_pad = jnp.concatenate([jnp.zeros((CHUNK - N_META, d), F32), meta_tokens.astype(F32)], axis=0)
    cos_m, sin_m = _rope_tables(jnp.arange(CHUNK) - (CHUNK - N_META))
    pa_m, pb_m, pc_m, z_last = _in_proj(meta_pad, g_mix, w_in_b, jnp.zeros((1, P_A), F32), cos_m, sin_m,
                                        mu, pvec, wwa_b, wg_b, bd)
    zeros_a = jnp.zeros((N_PAIRS, LANES, LANES), F32)
    zeros_b = jnp.zeros((H_B, HEAD_B, HEAD_B), F32)
    _, sa_meta, rb_meta = _mixer(pa_m[None], pb_m[None], pc_m[None], zeros_a, zeros_b, pvec, bd)

    x2d = x.reshape(t, d)
    cos_x, sin_x = _rope_tables(N_META + jnp.arange(seq))
    pa, pb, pc, _ = _in_proj(x2d, g_mix, w_in_b, z_last, cos_x, sin_x, mu, pvec, wwa_b, wg_b, bd)
    mixed, _, _ = _mixer(pa.reshape(bsz, seq, OPS_A), pb.reshape(bsz, seq, OPS_B),
                         pc.reshape(bsz, seq // CHUNK, 8, D_A), sa_meta[0], rb_meta[0], pvec, bd)

    tm_r = min(TM_ROUTE, t)
    n_log = N_GROUPS + N_EXPERTS
    wr = jnp.zeros((d, LANES), F32)
    wr = wr.at[:, :N_GROUPS].set(router_group_w[li]).at[:, N_GROUPS:n_log].set(router_expert_w[li])
    wr_cat = jnp.concatenate(_hi_lo(wr), axis=1)
    br = jnp.zeros((1, LANES), F32)
    br = br.at[0, :N_GROUPS].set(router_group_b[li]).at[0, N_GROUPS:n_log].set(router_expert_b[li])
    ii = jnp.arange(tm_r // ROUTER_PARTS)
    utri = (ii[:, None] < ii[None, :]).astype(BF16)
    h1, u2, route_t, counts = _out_router(mixed.reshape(t, d), x2d, w_out[li].astype(BF16), norm_ffn[li][None, :],
                                          wr_cat, br, utri)

    n_blocks = -(-(2 * t + N_EXPERTS * (BM_MOE - 1)) // BM_MOE)
    n_rows = n_blocks * BM_MOE
    cnt = counts[N_GROUPS:n_log, 0].astype(jnp.int32)
    padded = (cnt + BM_MOE - 1) // BM_MOE * BM_MOE
    pad_end = jnp.cumsum(padded)
    pad_start = pad_end - padded
    n_used = (pad_end[-1:] // BM_MOE).astype(jnp.int32)
    blk_start = jnp.arange(n_blocks, dtype=jnp.int32) * BM_MOE
    block_expert = jnp.minimum(jnp.sum(pad_end[None, :] <= blk_start[:, None], axis=1), N_EXPERTS - 1).astype(jnp.int32)
    eids = route_t[0:2].astype(jnp.int32)
    ranks = route_t[4:6].astype(jnp.int32)
    sel = eids[:, :, None] == jnp.arange(N_EXPERTS, dtype=jnp.int32)[None, None, :]
    dest = jnp.sum(jnp.where(sel, pad_start[None, None, :], 0), axis=-1) + ranks

    def tiled(tm):
        return dest.reshape(2, t // tm, tm).transpose(1, 0, 2)

    x_rows = _dispatch(pad_end.astype(jnp.int32), padded.astype(jnp.int32), tiled(tm_r), u2, n_rows)
    y_rows = _expert_ffn(block_expert, n_used, x_rows, moe_w_gate[li], moe_w_up[li], moe_w_down[li])
    out = _combine(tiled(min(TM_COMB, t)), h1, route_t, norm_final[None, :], y_rows)
    return out.reshape(bsz, seq, d)
```
